```python
import jax
import jax.numpy as jnp
from jax import lax
import numpy as np


D_MODEL = 4096
BATCH = 1
SEQ = 16384
DEPTH = 4

N_MIXERS = 3
N_A = (DEPTH + 2) // 3
N_B = (DEPTH + 1) // 3
N_C = DEPTH // 3

D_FF = (3 * D_MODEL) // 2
EPS = 1e-6

A_HEAD_DIM = 128
A_HEADS = D_MODEL // (2 * A_HEAD_DIM)
A_WIDTH = A_HEADS * A_HEAD_DIM
A_GROUPS = ((128, 1), (512, 4), (2048, 16))
A_N_GROUPS = len(A_GROUPS)
A_BLOCK = 64
ROPE_THETA = 500000.0
A_ROPE_DIMS = A_HEAD_DIM // 4

CONV_WIDTH = 31

C_HEADS = D_MODEL // 128
C_Q_RANK = D_MODEL // 4
C_KV_RANK = D_MODEL // 8
C_NOPE = 128
C_ROPE = 64
C_V = 128
C_ROPE_THETA = 10000.0
C_BLOCK = 128

NEG = -1e30

kernel_name = 'hybrid_dilated_conv_mla_encoder'


def _rmsnorm(x, g):
    xf = x.astype(jnp.float32)
    y = xf * lax.rsqrt(jnp.mean(xf * xf, axis=-1, keepdims=True) + EPS)
    return (y * g.astype(jnp.float32)).astype(x.dtype)


def _layernorm(x, g, b):
    xf = x.astype(jnp.float32)
    mu = jnp.mean(xf, axis=-1, keepdims=True)
    var = jnp.mean(jnp.square(xf - mu), axis=-1, keepdims=True)
    y = (xf - mu) * lax.rsqrt(var + EPS)
    return (y * g.astype(jnp.float32) + b.astype(jnp.float32)).astype(x.dtype)


def _rope(t, pos, theta, n_rot):
    half = n_rot // 2
    inv_freq = theta ** (-2.0 * jnp.arange(half, dtype=jnp.float32) / n_rot)
    ang = pos[:, None] * inv_freq[None, :]
    cos = jnp.cos(ang)[None, :, None, :]
    sin = jnp.sin(ang)[None, :, None, :]
    tf = t.astype(jnp.float32)
    t1, t2 = tf[..., :half], tf[..., half:n_rot]
    out = jnp.concatenate([t1 * cos - t2 * sin, t2 * cos + t1 * sin, tf[..., n_rot:]], axis=-1)
    return out.astype(t.dtype)


def _swiglu(h, w_gate, w_up, w_down):
    return (jax.nn.silu(h @ w_gate) * (h @ w_up)) @ w_down


def _strided_band_attention(q, k, v, half, dil):
    B, S, H, Dh = q.shape
    L = S // dil
    nb = -(-L // A_BLOCK)
    Lp = nb * A_BLOCK

    def sub(t):
        return t.reshape(B, L, dil, H, Dh).transpose(0, 2, 1, 3, 4)

    def key_blocks(t):
        tp = jnp.pad(sub(t), ((0, 0), (0, 0), (A_BLOCK, Lp - L + A_BLOCK), (0, 0), (0, 0)))
        tp = tp.reshape(B, dil, nb + 2, A_BLOCK, H, Dh)
        return jnp.concatenate([tp[:, :, :-2], tp[:, :, 1:-1], tp[:, :, 2:]], axis=3)

    qb = jnp.pad(sub(q), ((0, 0), (0, 0), (0, Lp - L), (0, 0), (0, 0))).reshape(B, dil, nb, A_BLOCK, H, Dh)
    kb = key_blocks(k)
    vb = key_blocks(v)
    blk = jnp.arange(nb)[:, None, None]
    qi = jnp.arange(A_BLOCK)[None, :, None]
    kj = jnp.arange(3 * A_BLOCK)[None, None, :]
    rel = kj - A_BLOCK - qi
    kidx = blk * A_BLOCK + kj - A_BLOCK
    valid = (jnp.abs(rel) <= half) & (kidx >= 0) & (kidx < L)
    s = jnp.einsum('brnqhd,brnkhd->brnhqk', qb, kb).astype(jnp.float32) * (Dh ** -0.5)
    s = jnp.where(valid[None, None, :, None], s, NEG)
    lse = jax.nn.logsumexp(s, axis=-1)
    p = jnp.exp(s - lse[..., None])
    o = jnp.einsum('brnhqk,brnkhd->brnqhd', p.astype(v.dtype), vb)
    o = o.reshape(B, dil, Lp, H, Dh)[:, :, :L].transpose(0, 2, 1, 3, 4).reshape(B, S, H, Dh)
    lse = lse.transpose(0, 1, 2, 4, 3).reshape(B, dil, Lp, H)[:, :, :L]
    lse = lse.transpose(0, 2, 1, 3).reshape(B, S, H)
    return o, lse


def _dilated_attention(h, w_qkv, w_o, pos):
    B, S, _ = h.shape
    qkv = (h @ w_qkv).reshape(B, S, A_N_GROUPS, 3, A_HEADS, A_HEAD_DIM)
    outs, lses = [], []
    for g, (window, dil) in enumerate(A_GROUPS):
        q = _rope(qkv[:, :, g, 0], pos, ROPE_THETA, A_ROPE_DIMS)
        k = _rope(qkv[:, :, g, 1], pos, ROPE_THETA, A_ROPE_DIMS)
        o, lse = _strided_band_attention(q, k, qkv[:, :, g, 2], window // (2 * dil), dil)
        outs.append(o)
        lses.append(lse)
    wts = jax.nn.softmax(jnp.stack(lses, axis=0), axis=0)
    o = jnp.einsum('gbsh,gbshd->bshd', wts.astype(h.dtype), jnp.stack(outs, axis=0))
    return o.reshape(B, S, A_WIDTH) @ w_o


def _conformer_conv(h, w_pw1, b_pw1, w_dw, b_dw, ln_g, ln_b, w_pw2, b_pw2):
    u = h @ w_pw1 + b_pw1
    u = u[..., :D_MODEL] * jax.nn.sigmoid(u[..., D_MODEL:])
    pad = CONV_WIDTH // 2
    u = lax.conv_general_dilated(u, w_dw[:, None, :], window_strides=(1,), padding=((pad, pad),),
                                 dimension_numbers=('NWC', 'WIO', 'NWC'),
                                 feature_group_count=D_MODEL) + b_dw
    u = jax.nn.silu(_layernorm(u, ln_g, ln_b))
    return u @ w_pw2 + b_pw2


def _mla(h, w_dq, q_norm, w_uq, w_dkv, kv_norm, w_ukv, w_o, pos):
    B, S, _ = h.shape
    q = (_rmsnorm(h @ w_dq, q_norm) @ w_uq).reshape(B, S, C_HEADS, C_NOPE + C_ROPE)
    q_nope = q[..., :C_NOPE]
    q_rope = _rope(q[..., C_NOPE:], pos, C_ROPE_THETA, C_ROPE)
    ckv = h @ w_dkv
    c = _rmsnorm(ckv[..., :C_KV_RANK], kv_norm)
    k_rope = _rope(ckv[..., C_KV_RANK:][:, :, None, :], pos, C_ROPE_THETA, C_ROPE)[:, :, 0]
    kv = (c @ w_ukv).reshape(B, S, C_HEADS, C_NOPE + C_V)
    k_nope, v = kv[..., :C_NOPE], kv[..., C_NOPE:]
    nb = S // C_BLOCK
    scale = (C_NOPE + C_ROPE) ** -0.5

    def blocks(t):
        return t.reshape((B, nb, C_BLOCK) + t.shape[2:]).swapaxes(0, 1)

    def attend(args):
        qn, qr = args
        s = jnp.einsum('bqhd,bkhd->bhqk', qn, k_nope) + jnp.einsum('bqhr,bkr->bhqk', qr, k_rope)
        p = jax.nn.softmax(s.astype(jnp.float32) * scale, axis=-1)
        return jnp.einsum('bhqk,bkhd->bqhd', p.astype(v.dtype), v)

    o = lax.map(attend, (blocks(q_nope), blocks(q_rope)))
    return o.swapaxes(0, 1).reshape(B, S, C_HEADS * C_V) @ w_o


def setup_inputs(seed: int = 0) -> dict:
    key = jax.random.key(seed)
    ks = iter(jax.random.split(key, 32))

    def w(shape, fan_in):
        return jax.random.normal(next(ks), shape, jnp.float32) * (fan_in ** -0.5)

    def gain(shape):
        return 1.0 + 0.02 * jax.random.normal(next(ks), shape, jnp.float32)

    def bias(shape):
        return 0.01 * jax.random.normal(next(ks), shape, jnp.float32)

    D, F = D_MODEL, D_FF
    return {
        'x': jax.random.normal(next(ks), (BATCH, SEQ, D), jnp.float32),
        'ffn1_norm': gain((DEPTH, D)),
        'ffn1_w_gate': w((DEPTH, D, F), D),
        'ffn1_w_up': w((DEPTH, D, F), D),
        'ffn1_w_down': w((DEPTH, F, D), F),
        'mix_norm': gain((DEPTH, D)),
        'ffn2_norm': gain((DEPTH, D)),
        'ffn2_w_gate': w((DEPTH, D, F), D),
        'ffn2_w_up': w((DEPTH, D, F), D),
        'ffn2_w_down': w((DEPTH, F, D), F),
        'final_norm': gain((D,)),
        'a_w_qkv': w((N_A, D, A_N_GROUPS * 3 * A_WIDTH), D),
        'a_w_o': w((N_A, A_WIDTH, D), A_WIDTH),
        'b_w_pw1': w((N_B, D, 2 * D), D),
        'b_b_pw1': bias((N_B, 2 * D)),
        'b_w_dw': w((N_B, CONV_WIDTH, D), CONV_WIDTH),
        'b_b_dw': bias((N_B, D)),
        'b_ln_g': gain((N_B, D)),
        'b_ln_b': bias((N_B, D)),
        'b_w_pw2': w((N_B, D, D), D),
        'b_b_pw2': bias((N_B, D)),
        'c_w_dq': w((N_C, D, C_Q_RANK), D),
        'c_q_norm': gain((N_C, C_Q_RANK)),
        'c_w_uq': w((N_C, C_Q_RANK, C_HEADS * (C_NOPE + C_ROPE)), C_Q_RANK),
        'c_w_dkv': w((N_C, D, C_KV_RANK + C_ROPE), D),
        'c_kv_norm': gain((N_C, C_KV_RANK)),
        'c_w_ukv': w((N_C, C_KV_RANK, C_HEADS * (C_NOPE + C_V)), C_KV_RANK),
        'c_w_o': w((N_C, C_HEADS * C_V, D), C_HEADS * C_V),
    }


def reference(x, ffn1_norm, ffn1_w_gate, ffn1_w_up, ffn1_w_down, mix_norm, ffn2_norm,
              ffn2_w_gate, ffn2_w_up, ffn2_w_down, final_norm, a_w_qkv, a_w_o,
              b_w_pw1, b_b_pw1, b_w_dw, b_b_dw, b_ln_g, b_ln_b, b_w_pw2, b_b_pw2,
              c_w_dq, c_q_norm, c_w_uq, c_w_dkv, c_kv_norm, c_w_ukv, c_w_o):
    pos = jnp.arange(x.shape[1], dtype=jnp.float32)
    for i in range(DEPTH):
        j = i // N_MIXERS
        x = x + 0.5 * _swiglu(_rmsnorm(x, ffn1_norm[i]), ffn1_w_gate[i], ffn1_w_up[i], ffn1_w_down[i])
        h = _rmsnorm(x, mix_norm[i])
        if i % N_MIXERS == 0:
            x = x + _dilated_attention(h, a_w_qkv[j], a_w_o[j], pos)
        elif i % N_MIXERS == 1:
            x = x + _conformer_conv(h, b_w_pw1[j], b_b_pw1[j], b_w_dw[j], b_b_dw[j],
                                    b_ln_g[j], b_ln_b[j], b_w_pw2[j], b_b_pw2[j])
        else:
            x = x + _mla(h, c_w_dq[j], c_q_norm[j], c_w_uq[j], c_w_dkv[j], c_kv_norm[j],
                         c_w_ukv[j], c_w_o[j], pos)
        x = x + 0.5 * _swiglu(_rmsnorm(x, ffn2_norm[i]), ffn2_w_gate[i], ffn2_w_up[i], ffn2_w_down[i])
    return _rmsnorm(x, final_norm)
```

```python
import functools
import math

import jax
import jax.numpy as jnp
from jax import lax
from jax.experimental import pallas as pl
from jax.experimental.pallas import tpu as pltpu

F32 = jnp.float32
BF16 = jnp.bfloat16

EPS = 1e-6
NEG = -1e30

N_MIXERS = 3
A_HEAD_DIM = 128
A_GROUPS = ((128, 1), (512, 4), (2048, 16))
A_ROPE_THETA = 500000.0
A_ROPE_DIMS = A_HEAD_DIM // 4
CONV_WIDTH = 31
C_NOPE = 128
C_ROPE = 64
C_V = 128
C_ROPE_THETA = 10000.0

V7X_VMEM_BYTES = 64 * 1024 * 1024
V7X_LANES = 128
V7X_SUBLANES = 8
COMPILER_SCRATCH_BYTES = 12 * 1024 * 1024

BAND_HALO = 64
BAND_SUB = 128
CONV_HALO = 16


def _params(semantics, *buffer_bytes):
    need = sum(buffer_bytes) + COMPILER_SCRATCH_BYTES
    return pltpu.CompilerParams(dimension_semantics=semantics,
                                vmem_limit_bytes=min(need, V7X_VMEM_BYTES - 2 * 1024 * 1024))


def _nbytes(shape, dtype):
    return math.prod(shape) * jnp.dtype(dtype).itemsize


def _tile(n, target, align):
    if n <= target:
        return n
    t = (target // align) * align
    while n % t:
        t -= align
    return t


def _sigmoid(x):
    return 1.0 / (1.0 + jnp.exp(-x))


def _rmsnorm_kernel(x_ref, g_ref, o_ref):
    x = x_ref[...].astype(F32)
    y = x * lax.rsqrt(jnp.mean(x * x, axis=-1, keepdims=True) + EPS)
    o_ref[...] = (y * g_ref[...]).astype(o_ref.dtype)


def _rmsnorm(x, g, out_dtype, tm=256):
    s, d = x.shape
    tm = min(tm, s)
    return pl.pallas_call(
        _rmsnorm_kernel,
        grid=(s // tm,),
        in_specs=[pl.BlockSpec((tm, d), lambda i: (i, 0)),
                  pl.BlockSpec((1, d), lambda i: (0, 0))],
        out_specs=pl.BlockSpec((tm, d), lambda i: (i, 0)),
        out_shape=jax.ShapeDtypeStruct((s, d), out_dtype),
        compiler_params=_params(("parallel",), 2 * _nbytes((tm, d), x.dtype), 2 * _nbytes((tm, d), out_dtype),
                                2 * _nbytes((tm, d), F32)),
        name="rmsnorm",
    )(x, g.reshape(1, d).astype(F32))


def _mm_kernel(*refs, has_bias, has_res, scale):
    x_ref, w_ref = refs[0], refs[1]
    pos = 2
    acc = jnp.dot(x_ref[...], w_ref[...], preferred_element_type=F32)
    if has_bias:
        acc = acc + refs[pos][...]
        pos += 1
    if has_res:
        acc = refs[pos][...] + scale * acc
        pos += 1
    o_ref = refs[pos]
    o_ref[...] = acc.astype(o_ref.dtype)


def _matmul(x, w, *, bias=None, residual=None, scale=1.0, out_dtype=F32, tm=1024, tn=512, name="matmul"):
    m, k = x.shape
    n = w.shape[1]
    tm, tn = _tile(m, tm, V7X_SUBLANES), _tile(n, tn, V7X_LANES)
    in_specs = [pl.BlockSpec((tm, k), lambda i, j: (i, 0)),
                pl.BlockSpec((k, tn), lambda i, j: (0, j))]
    args = [x, w]
    sizes = [2 * _nbytes((tm, k), x.dtype), 2 * _nbytes((k, tn), w.dtype), 2 * _nbytes((tm, tn), out_dtype),
             _nbytes((tm, tn), F32)]
    if bias is not None:
        in_specs.append(pl.BlockSpec((1, tn), lambda i, j: (0, j)))
        args.append(bias.reshape(1, n).astype(F32))
    if residual is not None:
        in_specs.append(pl.BlockSpec((tm, tn), lambda i, j: (i, j)))
        args.append(residual)
        sizes.append(2 * _nbytes((tm, tn), residual.dtype))
    return pl.pallas_call(
        functools.partial(_mm_kernel, has_bias=bias is not None, has_res=residual is not None, scale=scale),
        grid=(m // tm, n // tn),
        in_specs=in_specs,
        out_specs=pl.BlockSpec((tm, tn), lambda i, j: (i, j)),
        out_shape=jax.ShapeDtypeStruct((m, n), out_dtype),
        compiler_params=_params(("parallel", "arbitrary"), *sizes),
        name=name,
    )(*args)


def _gated_kernel(*refs, kind, has_bias):
    x_ref, w1_ref, w2_ref = refs[:3]
    x = x_ref[...]
    a = jnp.dot(x, w1_ref[...], preferred_element_type=F32)
    b = jnp.dot(x, w2_ref[...], preferred_element_type=F32)
    if has_bias:
        a = a + refs[3][...]
        b = b + refs[4][...]
    o_ref = refs[-1]
    if kind == "swiglu":
        out = (a * _sigmoid(a)) * b
    else:
        out = a * _sigmoid(b)
    o_ref[...] = out.astype(o_ref.dtype)


def _gated_matmul(x, w1, w2, *, kind, n_out, off2=0, b1=None, b2=None, out_dtype=BF16, tm=1024, tn=512,
                  name="gated"):
    m, k = x.shape
    tm, tn = _tile(m, tm, V7X_SUBLANES), _tile(n_out, tn, V7X_LANES)
    assert off2 % tn == 0
    ob = off2 // tn
    in_specs = [pl.BlockSpec((tm, k), lambda i, j: (i, 0)),
                pl.BlockSpec((k, tn), lambda i, j: (0, j)),
                pl.BlockSpec((k, tn), lambda i, j: (0, j + ob))]
    args = [x, w1, w2]
    has_bias = b1 is not None
    if has_bias:
        in_specs += [pl.BlockSpec((1, tn), lambda i, j: (0, j)),
                     pl.BlockSpec((1, tn), lambda i, j: (0, j + ob))]
        args += [b1.reshape(1, -1).astype(F32), b2.reshape(1, -1).astype(F32)]
    return pl.pallas_call(
        functools.partial(_gated_kernel, kind=kind, has_bias=has_bias),
        grid=(m // tm, n_out // tn),
        in_specs=in_specs,
        out_specs=pl.BlockSpec((tm, tn), lambda i, j: (i, j)),
        out_shape=jax.ShapeDtypeStruct((m, n_out), out_dtype),
        compiler_params=_params(("parallel", "arbitrary"), 2 * _nbytes((tm, k), x.dtype),
                                4 * _nbytes((k, tn), w1.dtype), 2 * _nbytes((tm, tn), out_dtype),
                                3 * _nbytes((tm, tn), F32)),
        name=name,
    )(*args)


def _rope_table_kernel(inv_ref, c_ref, sa_ref, sb_ref, *, tile, half):
    variant = pl.program_id(0)
    i = pl.program_id(1)
    shape = (tile, V7X_LANES)
    pos = (i * tile + lax.broadcasted_iota(jnp.int32, shape, 0)).astype(F32)
    lane = lax.broadcasted_iota(jnp.int32, shape, 1)
    ang = pos * inv_ref[...]
    cos, sin = jnp.cos(ang), jnp.sin(ang)
    rot = jnp.full(shape, variant, jnp.int32) == 0
    c_ref[...] = jnp.where(rot & (lane < 2 * half), cos, 1.0)
    sa_ref[...] = jnp.where(rot & (lane >= half) & (lane < 2 * half), sin, 0.0)
    sb_ref[...] = jnp.where(rot & (lane < half), -sin, 0.0)


def _rope_tables(seq, theta, n_rot, variants, tile=512):
    half = n_rot // 2
    inv_freq = theta ** (-2.0 * jnp.arange(half, dtype=F32) / n_rot)
    inv = jnp.concatenate([inv_freq, inv_freq, jnp.zeros((V7X_LANES - n_rot,), F32)]).reshape(1, V7X_LANES)
    tile = min(tile, seq)
    spec = pl.BlockSpec((None, tile, V7X_LANES), lambda v, i: (v, i, 0))
    shape = jax.ShapeDtypeStruct((variants, seq, V7X_LANES), F32)
    return pl.pallas_call(
        functools.partial(_rope_table_kernel, tile=tile, half=half),
        grid=(variants, seq // tile),
        in_specs=[pl.BlockSpec((1, V7X_LANES), lambda v, i: (0, 0))],
        out_specs=[spec, spec, spec],
        out_shape=[shape, shape, shape],
        compiler_params=_params(("parallel", "parallel"), 12 * _nbytes((tile, V7X_LANES), F32)),
        name="rope_tables",
    )(inv)


def _rope_slab(t, c, sa, sb, half):
    return t * c + pltpu.roll(t, half, 1) * sa + pltpu.roll(t, V7X_LANES - half, 1) * sb


def _qkv_rope_kernel(x_ref, w_ref, c_ref, sa_ref, sb_ref, o_ref, *, half):
    acc = jnp.dot(x_ref[...], w_ref[...], preferred_element_type=F32)
    c, sa, sb = c_ref[...], sa_ref[...], sb_ref[...]
    for s in range(acc.shape[1] // V7X_LANES):
        cols = slice(s * V7X_LANES, (s + 1) * V7X_LANES)
        o_ref[:, cols] = _rope_slab(acc[:, cols], c, sa, sb, half).astype(o_ref.dtype)


def _qkv_rope(x, w, tables, width, *, tm=1024, tn=1024):
    m, k = x.shape
    n = w.shape[1]
    tm, tn = _tile(m, tm, V7X_SUBLANES), _tile(width, tn, V7X_LANES)
    per = width // tn

    def tmap(i, j):
        return (jnp.where((j // per) % 3 == 2, 1, 0), i, 0)

    tspec = pl.BlockSpec((None, tm, V7X_LANES), tmap)
    return pl.pallas_call(
        functools.partial(_qkv_rope_kernel, half=A_ROPE_DIMS // 2),
        grid=(m // tm, n // tn),
        in_specs=[pl.BlockSpec((tm, k), lambda i, j: (i, 0)),
                  pl.BlockSpec((k, tn), lambda i, j: (0, j)),
                  tspec, tspec, tspec],
        out_specs=pl.BlockSpec((tm, tn), lambda i, j: (i, j)),
        out_shape=jax.ShapeDtypeStruct((m, n), BF16),
        compiler_params=_params(("parallel", "arbitrary"), 2 * _nbytes((tm, k), x.dtype),
                                2 * _nbytes((k, tn), w.dtype), 2 * _nbytes((tm, tn), BF16),
                                2 * _nbytes((tm, tn), F32), 6 * _nbytes((tm, V7X_LANES), F32)),
        name="qkv_rope",
    )(x, w, *tables)


def _band_kernel(q_ref, kp_ref, k_ref, kn_ref, vp_ref, v_ref, vn_ref, o_ref, lse_ref, *,
                 tq, n_heads, dh, half_w, sub_len, scale):
    i = pl.program_id(1)
    n_sub = tq // BAND_SUB
    win = BAND_SUB + 2 * BAND_HALO
    row = lax.broadcasted_iota(jnp.int32, (BAND_SUB, win), 0)
    col = lax.broadcasted_iota(jnp.int32, (BAND_SUB, win), 1)
    in_band = jnp.abs(col - BAND_HALO - row) <= half_w
    lane = lax.broadcasted_iota(jnp.int32, (BAND_SUB, V7X_LANES), 1)
    valid = []
    for j in range(n_sub):
        kpos = i * tq + (j * BAND_SUB - BAND_HALO) + col
        valid.append(in_band & (kpos >= 0) & (kpos < sub_len))
    lse_acc = [jnp.zeros((BAND_SUB, V7X_LANES), F32) for _ in range(n_sub)]
    for h in range(n_heads):
        hs = slice(h * dh, (h + 1) * dh)
        kc = jnp.concatenate([kp_ref[:, hs], k_ref[:, hs], kn_ref[:, hs]], axis=0)
        vc = jnp.concatenate([vp_ref[:, hs], v_ref[:, hs], vn_ref[:, hs]], axis=0)
        for j in range(n_sub):
            rows = slice(j * BAND_SUB, (j + 1) * BAND_SUB)
            q = q_ref[rows, hs]
            kw = kc[j * BAND_SUB:j * BAND_SUB + win]
            vw = vc[j * BAND_SUB:j * BAND_SUB + win]
            s = lax.dot_general(q, kw, (((1,), (1,)), ((), ())), preferred_element_type=F32) * scale
            s = jnp.where(valid[j], s, NEG)
            m = jnp.max(s, axis=-1, keepdims=True)
            e = jnp.exp(s - m)
            l = jnp.sum(e, axis=-1, keepdims=True)
            o = jnp.dot(e.astype(vw.dtype), vw, preferred_element_type=F32)
            o_ref[rows, hs] = o * (1.0 / l)
            lse_acc[j] = jnp.where(lane == h, m + jnp.log(l), lse_acc[j])
    for j in range(n_sub):
        lse_ref[j * BAND_SUB:(j + 1) * BAND_SUB, :] = lse_acc[j]


def _band_attention(qkv, group, dil, half_w, *, n_heads, dh, tq=256):
    s, w = qkv.shape
    width = n_heads * dh
    nblk = w // width
    sub_len = s // dil
    tq = min(tq, sub_len)
    assert sub_len % tq == 0 and tq % BAND_SUB == 0 and half_w <= BAND_HALO and n_heads <= V7X_LANES
    view = qkv.reshape(sub_len, dil * w)
    hb = tq // BAND_HALO
    last = sub_len // BAND_HALO - 1
    base = group * 3

    def main(sec):
        return pl.BlockSpec((tq, width), lambda r, i: (i, r * nblk + base + sec))

    def prev(sec):
        return pl.BlockSpec((BAND_HALO, width), lambda r, i: (jnp.maximum(i * hb - 1, 0), r * nblk + base + sec))

    def nxt(sec):
        return pl.BlockSpec((BAND_HALO, width), lambda r, i: (jnp.minimum((i + 1) * hb, last), r * nblk + base + sec))

    o, lse = pl.pallas_call(
        functools.partial(_band_kernel, tq=tq, n_heads=n_heads, dh=dh, half_w=half_w, sub_len=sub_len,
                          scale=dh ** -0.5),
        grid=(dil, sub_len // tq),
        in_specs=[main(0), prev(1), main(1), nxt(1), prev(2), main(2), nxt(2)],
        out_specs=[pl.BlockSpec((tq, width), lambda r, i: (i, r)),
                   pl.BlockSpec((tq, V7X_LANES), lambda r, i: (i, r))],
        out_shape=[jax.ShapeDtypeStruct((sub_len, dil * width), F32),
                   jax.ShapeDtypeStruct((sub_len, dil * V7X_LANES), F32)],
        compiler_params=_params(("parallel", "arbitrary"), 6 * _nbytes((tq + 2 * BAND_HALO, width), qkv.dtype),
                                2 * _nbytes((tq, width), F32), 2 * _nbytes((tq, V7X_LANES), F32)),
        name=f"band_attention_d{dil}",
    )(view, view, view, view, view, view, view)
    return o.reshape(s, width), lse.reshape(s, V7X_LANES)


def _merge_kernel(*refs, n_groups, n_heads, dh):
    o_refs, l_refs, out_ref = refs[:n_groups], refs[n_groups:2 * n_groups], refs[-1]
    ls = [r[...] for r in l_refs]
    m = functools.reduce(jnp.maximum, ls)
    es = [jnp.exp(l - m) for l in ls]
    inv = 1.0 / functools.reduce(lambda a, b: a + b, es)
    ws = [e * inv for e in es]
    for h in range(n_heads):
        hs = slice(h * dh, (h + 1) * dh)
        acc = ws[0][:, h:h + 1] * o_refs[0][:, hs]
        for g in range(1, n_groups):
            acc = acc + ws[g][:, h:h + 1] * o_refs[g][:, hs]
        out_ref[:, hs] = acc.astype(out_ref.dtype)


def _merge_groups(outs, lses, *, n_heads, dh, tm=256):
    s, width = outs[0].shape
    tm = min(tm, s)
    n = len(outs)
    ospec = pl.BlockSpec((tm, width), lambda i: (i, 0))
    lspec = pl.BlockSpec((tm, V7X_LANES), lambda i: (i, 0))
    return pl.pallas_call(
        functools.partial(_merge_kernel, n_groups=n, n_heads=n_heads, dh=dh),
        grid=(s // tm,),
        in_specs=[ospec] * n + [lspec] * n,
        out_specs=ospec,
        out_shape=jax.ShapeDtypeStruct((s, width), BF16),
        compiler_params=_params(("parallel",), 2 * n * _nbytes((tm, width), F32), 2 * _nbytes((tm, width), BF16),
                                2 * n * _nbytes((tm, V7X_LANES), F32)),
        name="merge_groups",
    )(*outs, *lses)


def _conv_ln_kernel(up_ref, u_ref, un_ref, w_ref, bdw_ref, g_ref, b_ref, o_ref, win_ref, y_ref, *,
                    ts, kw, rc, rl):
    i, j = pl.program_id(0), pl.program_id(1)
    n_i, n_j = pl.num_programs(0), pl.num_programs(1)
    cc = u_ref.shape[1]
    first = CONV_HALO - kw // 2
    win_ref[0:CONV_HALO, :] = jnp.where(i > 0, up_ref[...], 0.0)
    win_ref[CONV_HALO:CONV_HALO + ts, :] = u_ref[...]
    win_ref[CONV_HALO + ts:, :] = jnp.where(i < n_i - 1, un_ref[...], 0.0)
    for r0 in range(0, ts, rc):
        acc = jnp.broadcast_to(bdw_ref[...], (rc, cc))
        for k in range(kw):
            acc = acc + win_ref[r0 + first + k:r0 + first + k + rc, :] * w_ref[k:k + 1, :]
        y_ref[j, r0:r0 + rc, :] = acc

    @pl.when(j == n_j - 1)
    def _():
        n_c = y_ref.shape[0]
        inv_d = 1.0 / (n_c * cc)

        def ln_rows(rb, carry):
            r0 = pl.multiple_of(rb * rl, rl)
            ys = [y_ref[c, pl.ds(r0, rl), :] for c in range(n_c)]
            mu = functools.reduce(lambda a, b: a + b, [jnp.sum(y, axis=-1, keepdims=True) for y in ys]) * inv_d
            ycs = [y - mu for y in ys]
            var = functools.reduce(lambda a, b: a + b, [jnp.sum(y * y, axis=-1, keepdims=True) for y in ycs]) * inv_d
            inv = lax.rsqrt(var + EPS)
            for c in range(n_c):
                cols = slice(c * cc, (c + 1) * cc)
                z = ycs[c] * inv * g_ref[:, cols] + b_ref[:, cols]
                o_ref[pl.ds(r0, rl), cols] = (z * _sigmoid(z)).astype(o_ref.dtype)
            return carry

        lax.fori_loop(0, ts // rl, ln_rows, 0)


def _conv_ln_swish(u, w_dw, b_dw, ln_g, ln_b, *, ts=256, cc=512):
    s, d = u.shape
    kw = w_dw.shape[0]
    ts, cc = min(ts, s), min(cc, d)
    assert kw // 2 <= CONV_HALO and ts % CONV_HALO == 0 and s % ts == 0 and d % cc == 0
    hb = ts // CONV_HALO
    last = s // CONV_HALO - 1
    rc, rl = min(64, ts), min(16, ts)
    vec = pl.BlockSpec((1, d), lambda i, j: (0, 0))
    return pl.pallas_call(
        functools.partial(_conv_ln_kernel, ts=ts, kw=kw, rc=rc, rl=rl),
        grid=(s // ts, d // cc),
        in_specs=[pl.BlockSpec((CONV_HALO, cc), lambda i, j: (jnp.maximum(i * hb - 1, 0), j)),
                  pl.BlockSpec((ts, cc), lambda i, j: (i, j)),
                  pl.BlockSpec((CONV_HALO, cc), lambda i, j: (jnp.minimum((i + 1) * hb, last), j)),
                  pl.BlockSpec((kw, cc), lambda i, j: (0, j)),
                  pl.BlockSpec((1, cc), lambda i, j: (0, j)),
                  vec, vec],
        out_specs=pl.BlockSpec((ts, d), lambda i, j: (i, 0)),
        out_shape=jax.ShapeDtypeStruct((s, d), BF16),
        scratch_shapes=[pltpu.VMEM((ts + 2 * CONV_HALO, cc), F32), pltpu.VMEM((d // cc, ts, cc), F32)],
        compiler_params=_params(("parallel", "arbitrary"), 3 * _nbytes((ts + 2 * CONV_HALO, cc), F32),
                                2 * _nbytes((ts, d), BF16), _nbytes((ts, d), F32),
                                2 * _nbytes((kw + 3 * V7X_SUBLANES, d), F32)),
        name="conv_ln_swish",
    )(u, u, u, w_dw.astype(F32), b_dw.reshape(1, d).astype(F32), ln_g.reshape(1, d).astype(F32),
      ln_b.reshape(1, d).astype(F32))


def _mla_down_kernel(x_ref, wq_ref, wkv_ref, qn_ref, kvn_ref, c_ref, sa_ref, sb_ref, cq_ref, ckv_ref, kr_ref, *,
                     kv_rank, half):
    x = x_ref[...]
    a = jnp.dot(x, wq_ref[...], preferred_element_type=F32)
    a = a * lax.rsqrt(jnp.mean(a * a, axis=-1, keepdims=True) + EPS)
    cq_ref[...] = (a * qn_ref[...]).astype(cq_ref.dtype)
    b = jnp.dot(x, wkv_ref[...], preferred_element_type=F32)
    c = b[:, :kv_rank]
    c = c * lax.rsqrt(jnp.mean(c * c, axis=-1, keepdims=True) + EPS)
    ckv_ref[...] = (c * kvn_ref[...]).astype(ckv_ref.dtype)
    t = b[:, kv_rank:]
    kr_ref[...] = _rope_slab(t, c_ref[...], sa_ref[...], sb_ref[...], half).astype(kr_ref.dtype)


def _mla_down(x, w_dq, q_norm, w_dkv_pad, kv_norm, tables, kv_rank, *, tm=512):
    m, k = x.shape
    qr = w_dq.shape[1]
    nkv = w_dkv_pad.shape[1]
    tm = min(tm, m)
    tspec = pl.BlockSpec((None, tm, V7X_LANES), lambda i: (0, i, 0))
    return pl.pallas_call(
        functools.partial(_mla_down_kernel, kv_rank=kv_rank, half=C_ROPE // 2),
        grid=(m // tm,),
        in_specs=[pl.BlockSpec((tm, k), lambda i: (i, 0)),
                  pl.BlockSpec((k, qr), lambda i: (0, 0)),
                  pl.BlockSpec((k, nkv), lambda i: (0, 0)),
                  pl.BlockSpec((1, qr), lambda i: (0, 0)),
                  pl.BlockSpec((1, kv_rank), lambda i: (0, 0)),
                  tspec, tspec, tspec],
        out_specs=[pl.BlockSpec((tm, qr), lambda i: (i, 0)),
                   pl.BlockSpec((tm, kv_rank), lambda i: (i, 0)),
                   pl.BlockSpec((tm, V7X_LANES), lambda i: (i, 0))],
        out_shape=[jax.ShapeDtypeStruct((m, qr), BF16),
                   jax.ShapeDtypeStruct((m, kv_rank), BF16),
                   jax.ShapeDtypeStruct((m, V7X_LANES), BF16)],
        compiler_params=_params(("parallel",), 2 * _nbytes((tm, k), x.dtype), 2 * _nbytes((k, qr + nkv), BF16),
                                2 * _nbytes((tm, qr + nkv), BF16), 2 * _nbytes((tm, qr + nkv), F32),
                                6 * _nbytes((tm, V7X_LANES), F32)),
        name="mla_down",
    )(x, w_dq, w_dkv_pad, q_norm.reshape(1, qr).astype(F32), kv_norm.reshape(1, kv_rank).astype(F32), *tables)


def _mla_q_up_kernel(cq_ref, w_ref, c_ref, sa_ref, sb_ref, q_ref, *, half):
    r = jnp.dot(cq_ref[...], w_ref[...], preferred_element_type=F32)
    q_ref[:, :C_NOPE] = r[:, :C_NOPE].astype(q_ref.dtype)
    q_ref[:, C_NOPE:] = _rope_slab(r[:, C_NOPE:], c_ref[...], sa_ref[...], sb_ref[...], half).astype(q_ref.dtype)


def _mla_q_up(cq, w_uq_pad, tables, n_heads, *, tm=1024):
    m, k = cq.shape
    hw = w_uq_pad.shape[1] // n_heads
    tm = min(tm, m)
    tspec = pl.BlockSpec((None, tm, V7X_LANES), lambda i, h: (0, i, 0))
    return pl.pallas_call(
        functools.partial(_mla_q_up_kernel, half=C_ROPE // 2),
        grid=(m // tm, n_heads),
        in_specs=[pl.BlockSpec((tm, k), lambda i, h: (i, 0)),
                  pl.BlockSpec((k, hw), lambda i, h: (0, h)),
                  tspec, tspec, tspec],
        out_specs=pl.BlockSpec((None, tm, hw), lambda i, h: (h, i, 0)),
        out_shape=jax.ShapeDtypeStruct((n_heads, m, hw), BF16),
        compiler_params=_params(("parallel", "arbitrary"), 2 * _nbytes((tm, k), BF16), 2 * _nbytes((k, hw), BF16),
                                2 * _nbytes((tm, hw), BF16), 2 * _nbytes((tm, hw), F32),
                                6 * _nbytes((tm, V7X_LANES), F32)),
        name="mla_q_up",
    )(cq, w_uq_pad, *tables)


def _mla_kv_up_kernel(c_ref, w_ref, kr_ref, k_ref, v_ref):
    r = jnp.dot(c_ref[...], w_ref[...], preferred_element_type=F32)
    k_ref[:, :C_NOPE] = r[:, :C_NOPE].astype(k_ref.dtype)
    k_ref[:, C_NOPE:] = kr_ref[...]
    v_ref[...] = r[:, C_NOPE:].astype(v_ref.dtype)


def _mla_kv_up(c, w_ukv, kr, n_heads, *, tm=1024):
    m, k = c.shape
    hw = w_ukv.shape[1] // n_heads
    tm = min(tm, m)
    return pl.pallas_call(
        _mla_kv_up_kernel,
        grid=(m // tm, n_heads),
        in_specs=[pl.BlockSpec((tm, k), lambda i, h: (i, 0)),
                  pl.BlockSpec((k, hw), lambda i, h: (0, h)),
                  pl.BlockSpec((tm, V7X_LANES), lambda i, h: (i, 0))],
        out_specs=[pl.BlockSpec((None, tm, C_NOPE + V7X_LANES), lambda i, h: (h, i, 0)),
                   pl.BlockSpec((None, tm, C_V), lambda i, h: (h, i, 0))],
        out_shape=[jax.ShapeDtypeStruct((n_heads, m, C_NOPE + V7X_LANES), BF16),
                   jax.ShapeDtypeStruct((n_heads, m, C_V), BF16)],
        compiler_params=_params(("parallel", "arbitrary"), 2 * _nbytes((tm, k), BF16), 2 * _nbytes((k, hw), BF16),
                                2 * _nbytes((tm, 3 * V7X_LANES), BF16), 2 * _nbytes((tm, hw), F32),
                                2 * _nbytes((tm, V7X_LANES), BF16)),
        name="mla_kv_up",
    )(c, w_ukv, kr)


def _flash_kernel(q_ref, k_ref, v_ref, o_ref, m_ref, l_ref, acc_ref, *, tk, scale):
    n_kv = k_ref.shape[0] // tk
    m_ref[...] = jnp.full(m_ref.shape, NEG, F32)
    l_ref[...] = jnp.zeros(l_ref.shape, F32)
    acc_ref[...] = jnp.zeros(acc_ref.shape, F32)
    q = q_ref[...]

    def step(kk, carry):
        off = pl.multiple_of(kk * tk, tk)
        k = k_ref[pl.ds(off, tk), :]
        v = v_ref[pl.ds(off, tk), :]
        s = lax.dot_general(q, k, (((1,), (1,)), ((), ())), preferred_element_type=F32) * scale
        m_prev = m_ref[...]
        m_new = jnp.maximum(m_prev, jnp.max(s, axis=-1, keepdims=True))
        alpha = jnp.exp(m_prev - m_new)
        p = jnp.exp(s - m_new[:, :1])
        l_ref[...] = alpha * l_ref[...] + jnp.sum(p, axis=-1, keepdims=True)
        acc_ref[...] = alpha * acc_ref[...] + jnp.dot(p.astype(v.dtype), v, preferred_element_type=F32)
        m_ref[...] = m_new
        return carry

    lax.fori_loop(0, n_kv, step, 0)
    o_ref[...] = (acc_ref[...] * (1.0 / l_ref[...])).astype(o_ref.dtype)


def _flash_attention(q, k, v, scale, *, tq=512, tk=512):
    n_heads, s, dk = q.shape
    dv = v.shape[2]
    tq, tk = min(tq, s), min(tk, s)
    assert dv == V7X_LANES
    return pl.pallas_call(
        functools.partial(_flash_kernel, tk=tk, scale=scale),
        grid=(n_heads, s // tq),
        in_specs=[pl.BlockSpec((None, tq, dk), lambda h, i: (h, i, 0)),
                  pl.BlockSpec((None, s, dk), lambda h, i: (h, 0, 0)),
                  pl.BlockSpec((None, s, dv), lambda h, i: (h, 0, 0))],
        out_specs=pl.BlockSpec((tq, dv), lambda h, i: (i, h)),
        out_shape=jax.ShapeDtypeStruct((s, n_heads * dv), BF16),
        scratch_shapes=[pltpu.VMEM((tq, V7X_LANES), F32), pltpu.VMEM((tq, V7X_LANES), F32),
                        pltpu.VMEM((tq, dv), F32)],
        compiler_params=_params(("parallel", "arbitrary"), 2 * _nbytes((tq, dk), BF16),
                                2 * _nbytes((s, dk + dv), BF16), 2 * _nbytes((tq, dv), BF16),
                                3 * _nbytes((tq, V7X_LANES), F32), 3 * _nbytes((tq, tk), F32)),
        name="mla_flash_attention",
    )(q, k, v)


def _ffn(x, norm_g, w_gate, w_up, w_down):
    h = _rmsnorm(x, norm_g, BF16)
    a = _gated_matmul(h, w_gate.astype(BF16), w_up.astype(BF16), kind="swiglu", n_out=w_gate.shape[1],
                      name="ffn_gate_up")
    return _matmul(a, w_down.astype(BF16), residual=x, scale=0.5, name="ffn_down")


def _mixer_a(x, h, w_qkv, w_o, tables):
    d = x.shape[1]
    n_heads = d // (2 * A_HEAD_DIM)
    width = n_heads * A_HEAD_DIM
    qkv = _qkv_rope(h, w_qkv.astype(BF16), tables, width)
    outs, lses = [], []
    for g, (window, dil) in enumerate(A_GROUPS):
        o, lse = _band_attention(qkv, g, dil, window // (2 * dil), n_heads=n_heads, dh=A_HEAD_DIM)
        outs.append(o)
        lses.append(lse)
    merged = _merge_groups(outs, lses, n_heads=n_heads, dh=A_HEAD_DIM)
    return _matmul(merged, w_o.astype(BF16), residual=x, name="a_out_proj")


def _mixer_b(x, h, w_pw1, b_pw1, w_dw, b_dw, ln_g, ln_b, w_pw2, b_pw2):
    d = x.shape[1]
    w1 = w_pw1.astype(BF16)
    u = _gated_matmul(h, w1, w1, kind="glu", n_out=d, off2=d, b1=b_pw1, b2=b_pw1, out_dtype=F32, name="conv_pw1_glu")
    v = _conv_ln_swish(u, w_dw, b_dw, ln_g, ln_b)
    return _matmul(v, w_pw2.astype(BF16), bias=b_pw2, residual=x, name="conv_pw2")


def _mixer_c(x, h, w_dq, q_norm, w_uq, w_dkv, kv_norm, w_ukv, w_o, tables):
    d = x.shape[1]
    n_heads = d // C_V
    kv_rank = w_dkv.shape[1] - C_ROPE
    q_rank = w_dq.shape[1]
    pad = V7X_LANES - C_ROPE
    w_dkv_pad = jnp.pad(w_dkv, ((0, 0), (0, pad))).astype(BF16)
    w_uq_pad = jnp.pad(w_uq.reshape(q_rank, n_heads, C_NOPE + C_ROPE), ((0, 0), (0, 0), (0, pad)))
    w_uq_pad = w_uq_pad.reshape(q_rank, n_heads * (C_NOPE + V7X_LANES)).astype(BF16)
    cq, ckv, kr = _mla_down(h, w_dq.astype(BF16), q_norm, w_dkv_pad, kv_norm, tables, kv_rank)
    q = _mla_q_up(cq, w_uq_pad, tables, n_heads)
    k, v = _mla_kv_up(ckv, w_ukv.astype(BF16), kr, n_heads)
    o = _flash_attention(q, k, v, (C_NOPE + C_ROPE) ** -0.5)
    return _matmul(o, w_o.astype(BF16), residual=x, name="c_out_proj")


def kernel(x, ffn1_norm, ffn1_w_gate, ffn1_w_up, ffn1_w_down, mix_norm, ffn2_norm, ffn2_w_gate, ffn2_w_up,
           ffn2_w_down, final_norm, a_w_qkv, a_w_o, b_w_pw1, b_b_pw1, b_w_dw, b_b_dw, b_ln_g, b_ln_b, b_w_pw2,
           b_b_pw2, c_w_dq, c_q_norm, c_w_uq, c_w_dkv, c_kv_norm, c_w_ukv, c_w_o):
    batch, seq, d = x.shape
    depth = ffn1_norm.shape[0]
    tables_a = _rope_tables(seq, A_ROPE_THETA, A_ROPE_DIMS, 2)
    tables_c = _rope_tables(seq, C_ROPE_THETA, C_ROPE, 1) if depth >= N_MIXERS else None
    outs = []
    for b in range(batch):
        xb = x[b]
        for i in range(depth):
            j = i // N_MIXERS
            xb = _ffn(xb, ffn1_norm[i], ffn1_w_gate[i], ffn1_w_up[i], ffn1_w_down[i])
            h = _rmsnorm(xb, mix_norm[i], BF16)
            if i % N_MIXERS == 0:
                xb = _mixer_a(xb, h, a_w_qkv[j], a_w_o[j], tables_a)
            elif i % N_MIXERS == 1:
                xb = _mixer_b(xb, h, b_w_pw1[j], b_b_pw1[j], b_w_dw[j], b_b_dw[j], b_ln_g[j], b_ln_b[j],
                              b_w_pw2[j], b_b_pw2[j])
            else:
                xb = _mixer_c(xb, h, c_w_dq[j], c_q_norm[j], c_w_uq[j], c_w_dkv[j], c_kv_norm[j], c_w_ukv[j],
                              c_w_o[j], tables_c)
            xb = _ffn(xb, ffn2_norm[i], ffn2_w_gate[i], ffn2_w_up[i], ffn2_w_down[i])
        outs.append(_rmsnorm(xb, final_norm, x.dtype))
    return jnp.stack(outs, axis=0)
```

```python
import functools
import math

import jax
import jax.numpy as jnp
from jax import lax
from jax.experimental import pallas as pl
from jax.experimental.pallas import tpu as pltpu

F32 = jnp.float32
BF16 = jnp.bfloat16

EPS = 1e-6
NEG = -1e30

N_MIXERS = 3
A_HEAD_DIM = 128
A_GROUPS = ((128, 1), (512, 4), (2048, 16))
A_ROPE_THETA = 500000.0
A_ROPE_DIMS = A_HEAD_DIM // 4
CONV_WIDTH = 31
C_NOPE = 128
C_ROPE = 64
C_V = 128
C_ROPE_THETA = 10000.0

V7X_VMEM_BYTES = 64 * 1024 * 1024
V7X_LANES = 128
V7X_SUBLANES = 8
COMPILER_SCRATCH_BYTES = 12 * 1024 * 1024

BAND_HALO = 64
BAND_SUB = 128
CONV_HALO = 16
FLASH_KEY_CHUNK = 512


def _params(semantics, *buffer_bytes):
    need = sum(buffer_bytes) + COMPILER_SCRATCH_BYTES
    return pltpu.CompilerParams(dimension_semantics=semantics,
                                vmem_limit_bytes=min(need, V7X_VMEM_BYTES - 2 * 1024 * 1024))


def _nbytes(shape, dtype):
    return math.prod(shape) * jnp.dtype(dtype).itemsize


def _tile(n, target, align):
    if n <= target:
        return n
    t = (target // align) * align
    while n % t:
        t -= align
    return t


def _sigmoid(x):
    return 1.0 / (1.0 + jnp.exp(-x))


def _rmsnorm_kernel(x_ref, g_ref, o_ref):
    x = x_ref[...].astype(F32)
    y = x * lax.rsqrt(jnp.mean(x * x, axis=-1, keepdims=True) + EPS)
    o_ref[...] = (y * g_ref[...]).astype(o_ref.dtype)


def _rmsnorm(x, g, out_dtype, tm=256):
    s, d = x.shape
    tm = min(tm, s)
    return pl.pallas_call(
        _rmsnorm_kernel,
        grid=(s // tm,),
        in_specs=[pl.BlockSpec((tm, d), lambda i: (i, 0)),
                  pl.BlockSpec((1, d), lambda i: (0, 0))],
        out_specs=pl.BlockSpec((tm, d), lambda i: (i, 0)),
        out_shape=jax.ShapeDtypeStruct((s, d), out_dtype),
        compiler_params=_params(("parallel",), 2 * _nbytes((tm, d), x.dtype), 2 * _nbytes((tm, d), out_dtype),
                                2 * _nbytes((tm, d), F32)),
        name="rmsnorm",
    )(x, g.reshape(1, d).astype(F32))


def _rmsnorm_strided_kernel(x_ref, g_ref, *refs, dils):
    n_perm = sum(1 for dil in dils if dil > 1)
    p_refs, o_refs = refs[:n_perm], refs[n_perm:]
    x = x_ref[...].astype(F32)
    y = x * lax.rsqrt(jnp.mean(x * x, axis=-1, keepdims=True) + EPS)
    y = (y * g_ref[...]).astype(BF16)
    tm = y.shape[0]
    p_iter = iter(p_refs)
    for o_ref, dil in zip(o_refs, dils):
        if dil == 1:
            o_ref[0] = y
            continue
        yp = jnp.dot(next(p_iter)[...], y, preferred_element_type=F32).astype(BF16)
        n = tm // dil
        for r in range(dil):
            o_ref[r] = yp[r * n:(r + 1) * n]


def _rmsnorm_strided(x, g, dils, tm=256):
    s, d = x.shape
    tm = min(tm, s)
    perms, pspecs, out_specs, out_shapes = [], [], [], []
    for dil in dils:
        n = tm // dil
        assert s % tm == 0 and (dil == 1 or (tm % dil == 0 and n % 16 == 0))
        if dil > 1:
            rows = jnp.arange(tm)
            src = (rows % n) * dil + rows // n
            perms.append((src[:, None] == jnp.arange(tm)[None, :]).astype(BF16))
            pspecs.append(pl.BlockSpec((tm, tm), lambda i: (0, 0)))
        out_specs.append(pl.BlockSpec((dil, n, d), lambda i: (0, i, 0)))
        out_shapes.append(jax.ShapeDtypeStruct((dil, s // dil, d), BF16))
    return pl.pallas_call(
        functools.partial(_rmsnorm_strided_kernel, dils=tuple(dils)),
        grid=(s // tm,),
        in_specs=[pl.BlockSpec((tm, d), lambda i: (i, 0)),
                  pl.BlockSpec((1, d), lambda i: (0, 0))] + pspecs,
        out_specs=out_specs,
        out_shape=out_shapes,
        compiler_params=_params(("parallel",), 2 * _nbytes((tm, d), x.dtype),
                                2 * len(dils) * _nbytes((tm, d), BF16), 3 * _nbytes((tm, d), F32)),
        name="rmsnorm_strided",
    )(x, g.reshape(1, d).astype(F32), *perms)


def _mm_kernel(*refs, has_bias, has_res, scale):
    x_ref, w_ref = refs[0], refs[1]
    pos = 2
    acc = jnp.dot(x_ref[...], w_ref[...], preferred_element_type=F32)
    if has_bias:
        acc = acc + refs[pos][...]
        pos += 1
    if has_res:
        acc = refs[pos][...] + scale * acc
        pos += 1
    o_ref = refs[pos]
    o_ref[...] = acc.astype(o_ref.dtype)


def _matmul(x, w, *, bias=None, residual=None, scale=1.0, out_dtype=F32, tm=1024, tn=512, name="matmul"):
    m, k = x.shape
    n = w.shape[1]
    tm, tn = _tile(m, tm, V7X_SUBLANES), _tile(n, tn, V7X_LANES)
    in_specs = [pl.BlockSpec((tm, k), lambda i, j: (i, 0)),
                pl.BlockSpec((k, tn), lambda i, j: (0, j))]
    args = [x, w]
    sizes = [2 * _nbytes((tm, k), x.dtype), 2 * _nbytes((k, tn), w.dtype), 2 * _nbytes((tm, tn), out_dtype),
             _nbytes((tm, tn), F32)]
    if bias is not None:
        in_specs.append(pl.BlockSpec((1, tn), lambda i, j: (0, j)))
        args.append(bias.reshape(1, n).astype(F32))
    if residual is not None:
        in_specs.append(pl.BlockSpec((tm, tn), lambda i, j: (i, j)))
        args.append(residual)
        sizes.append(2 * _nbytes((tm, tn), residual.dtype))
    return pl.pallas_call(
        functools.partial(_mm_kernel, has_bias=bias is not None, has_res=residual is not None, scale=scale),
        grid=(m // tm, n // tn),
        in_specs=in_specs,
        out_specs=pl.BlockSpec((tm, tn), lambda i, j: (i, j)),
        out_shape=jax.ShapeDtypeStruct((m, n), out_dtype),
        compiler_params=_params(("parallel", "arbitrary"), *sizes),
        name=name,
    )(*args)


def _gated_kernel(*refs, kind, has_bias):
    x_ref, w1_ref, w2_ref = refs[:3]
    x = x_ref[...]
    a = jnp.dot(x, w1_ref[...], preferred_element_type=F32)
    b = jnp.dot(x, w2_ref[...], preferred_element_type=F32)
    if has_bias:
        a = a + refs[3][...]
        b = b + refs[4][...]
    o_ref = refs[-1]
    if kind == "swiglu":
        out = (a * _sigmoid(a)) * b
    else:
        out = a * _sigmoid(b)
    o_ref[...] = out.astype(o_ref.dtype)


def _gated_matmul(x, w1, w2, *, kind, n_out, off2=0, b1=None, b2=None, out_dtype=BF16, tm=1024, tn=512,
                  name="gated"):
    m, k = x.shape
    tm, tn = _tile(m, tm, V7X_SUBLANES), _tile(n_out, tn, V7X_LANES)
    assert off2 % tn == 0
    ob = off2 // tn
    in_specs = [pl.BlockSpec((tm, k), lambda i, j: (i, 0)),
                pl.BlockSpec((k, tn), lambda i, j: (0, j)),
                pl.BlockSpec((k, tn), lambda i, j: (0, j + ob))]
    args = [x, w1, w2]
    has_bias = b1 is not None
    if has_bias:
        in_specs += [pl.BlockSpec((1, tn), lambda i, j: (0, j)),
                     pl.BlockSpec((1, tn), lambda i, j: (0, j + ob))]
        args += [b1.reshape(1, -1).astype(F32), b2.reshape(1, -1).astype(F32)]
    return pl.pallas_call(
        functools.partial(_gated_kernel, kind=kind, has_bias=has_bias),
        grid=(m // tm, n_out // tn),
        in_specs=in_specs,
        out_specs=pl.BlockSpec((tm, tn), lambda i, j: (i, j)),
        out_shape=jax.ShapeDtypeStruct((m, n_out), out_dtype),
        compiler_params=_params(("parallel", "arbitrary"), 2 * _nbytes((tm, k), x.dtype),
                                4 * _nbytes((k, tn), w1.dtype), 2 * _nbytes((tm, tn), out_dtype),
                                3 * _nbytes((tm, tn), F32)),
        name=name,
    )(*args)


def _rope_table_kernel(inv_ref, c_ref, sa_ref, sb_ref, *, tile, half, dil, sub_len):
    variant = pl.program_id(0)
    i = pl.program_id(1)
    shape = (tile, V7X_LANES)
    r = (i * tile) // sub_len
    l0 = (i * tile) % sub_len
    pos = ((l0 + lax.broadcasted_iota(jnp.int32, shape, 0)) * dil + r).astype(F32)
    lane = lax.broadcasted_iota(jnp.int32, shape, 1)
    ang = pos * inv_ref[...]
    cos, sin = jnp.cos(ang), jnp.sin(ang)
    rot = jnp.full(shape, variant, jnp.int32) == 0
    c_ref[...] = jnp.where(rot & (lane < 2 * half), cos, 1.0)
    sa_ref[...] = jnp.where(rot & (lane >= half) & (lane < 2 * half), sin, 0.0)
    sb_ref[...] = jnp.where(rot & (lane < half), -sin, 0.0)


def _rope_tables(seq, theta, n_rot, variants, dil=1, tile=512):
    half = n_rot // 2
    inv_freq = theta ** (-2.0 * jnp.arange(half, dtype=F32) / n_rot)
    inv = jnp.concatenate([inv_freq, inv_freq, jnp.zeros((V7X_LANES - n_rot,), F32)]).reshape(1, V7X_LANES)
    sub_len = seq // dil
    tile = min(tile, sub_len)
    assert sub_len % tile == 0
    spec = pl.BlockSpec((None, tile, V7X_LANES), lambda v, i: (v, i, 0))
    shape = jax.ShapeDtypeStruct((variants, seq, V7X_LANES), F32)
    return pl.pallas_call(
        functools.partial(_rope_table_kernel, tile=tile, half=half, dil=dil, sub_len=sub_len),
        grid=(variants, seq // tile),
        in_specs=[pl.BlockSpec((1, V7X_LANES), lambda v, i: (0, 0))],
        out_specs=[spec, spec, spec],
        out_shape=[shape, shape, shape],
        compiler_params=_params(("parallel", "parallel"), 12 * _nbytes((tile, V7X_LANES), F32)),
        name="rope_tables",
    )(inv)


def _rope_slab(t, c, sa, sb, half):
    return t * c + pltpu.roll(t, half, 1) * sa + pltpu.roll(t, V7X_LANES - half, 1) * sb


def _qkv_rope_kernel(x_ref, w_ref, c_ref, sa_ref, sb_ref, o_ref, *, half):
    acc = jnp.dot(x_ref[...], w_ref[...], preferred_element_type=F32)
    c, sa, sb = c_ref[...], sa_ref[...], sb_ref[...]
    for s in range(acc.shape[1] // V7X_LANES):
        cols = slice(s * V7X_LANES, (s + 1) * V7X_LANES)
        o_ref[:, cols] = _rope_slab(acc[:, cols], c, sa, sb, half).astype(o_ref.dtype)


def _qkv_rope(x, w, tables, width, group, *, tm=1024, tn=1024):
    m, k = x.shape
    n = 3 * width
    tm, tn = _tile(m, tm, V7X_SUBLANES), _tile(width, tn, V7X_LANES)
    per = width // tn
    col0 = group * 3 * per

    def tmap(i, j):
        return (jnp.where(j // per == 2, 1, 0), i, 0)

    tspec = pl.BlockSpec((None, tm, V7X_LANES), tmap)
    return pl.pallas_call(
        functools.partial(_qkv_rope_kernel, half=A_ROPE_DIMS // 2),
        grid=(m // tm, n // tn),
        in_specs=[pl.BlockSpec((tm, k), lambda i, j: (i, 0)),
                  pl.BlockSpec((k, tn), lambda i, j: (0, j + col0)),
                  tspec, tspec, tspec],
        out_specs=pl.BlockSpec((tm, tn), lambda i, j: (i, j)),
        out_shape=jax.ShapeDtypeStruct((m, n), BF16),
        compiler_params=_params(("parallel", "arbitrary"), 2 * _nbytes((tm, k), x.dtype),
                                2 * _nbytes((k, tn), w.dtype), 2 * _nbytes((tm, tn), BF16),
                                2 * _nbytes((tm, tn), F32), 6 * _nbytes((tm, V7X_LANES), F32)),
        name="qkv_rope",
    )(x, w, *tables)


def _band_kernel(q_ref, kp_ref, k_ref, kn_ref, vp_ref, v_ref, vn_ref, o_ref, lse_ref, *,
                 tq, n_heads, dh, half_w, sub_len, scale):
    i = pl.program_id(1)
    n_sub = tq // BAND_SUB
    win = BAND_SUB + 2 * BAND_HALO
    row = lax.broadcasted_iota(jnp.int32, (BAND_SUB, win), 0)
    col = lax.broadcasted_iota(jnp.int32, (BAND_SUB, win), 1)
    in_band = jnp.abs(col - BAND_HALO - row) <= half_w
    lane = lax.broadcasted_iota(jnp.int32, (BAND_SUB, V7X_LANES), 1)
    valid = []
    for j in range(n_sub):
        kpos = i * tq + (j * BAND_SUB - BAND_HALO) + col
        valid.append(in_band & (kpos >= 0) & (kpos < sub_len))
    lse_acc = [jnp.zeros((BAND_SUB, V7X_LANES), F32) for _ in range(n_sub)]
    for h in range(n_heads):
        hs = slice(h * dh, (h + 1) * dh)
        kc = jnp.concatenate([kp_ref[:, hs], k_ref[:, hs], kn_ref[:, hs]], axis=0)
        vc = jnp.concatenate([vp_ref[:, hs], v_ref[:, hs], vn_ref[:, hs]], axis=0)
        for j in range(n_sub):
            rows = slice(j * BAND_SUB, (j + 1) * BAND_SUB)
            q = q_ref[rows, hs]
            kw = kc[j * BAND_SUB:j * BAND_SUB + win]
            vw = vc[j * BAND_SUB:j * BAND_SUB + win]
            s = lax.dot_general(q, kw, (((1,), (1,)), ((), ())), preferred_element_type=F32) * scale
            s = jnp.where(valid[j], s, NEG)
            m = jnp.max(s, axis=-1, keepdims=True)
            e = jnp.exp(s - m)
            l = jnp.sum(e, axis=-1, keepdims=True)
            o = jnp.dot(e.astype(vw.dtype), vw, preferred_element_type=F32)
            o_ref[rows, hs] = o * (1.0 / l)
            lse_acc[j] = jnp.where(lane == h, m + jnp.log(l), lse_acc[j])
    for j in range(n_sub):
        lse_ref[j * BAND_SUB:(j + 1) * BAND_SUB, :] = lse_acc[j]


def _band_attention(qkv, dil, half_w, *, n_heads, dh, tq=256):
    dil_, sub_len, w = qkv.shape
    width = n_heads * dh
    tq = min(tq, sub_len)
    assert dil_ == dil and w == 3 * width
    assert sub_len % tq == 0 and tq % BAND_SUB == 0 and half_w <= BAND_HALO and n_heads <= V7X_LANES
    hb = tq // BAND_HALO
    last = sub_len // BAND_HALO - 1

    def main(sec):
        return pl.BlockSpec((None, tq, width), lambda r, i: (r, i, sec))

    def prev(sec):
        return pl.BlockSpec((None, BAND_HALO, width), lambda r, i: (r, jnp.maximum(i * hb - 1, 0), sec))

    def nxt(sec):
        return pl.BlockSpec((None, BAND_HALO, width), lambda r, i: (r, jnp.minimum((i + 1) * hb, last), sec))

    return pl.pallas_call(
        functools.partial(_band_kernel, tq=tq, n_heads=n_heads, dh=dh, half_w=half_w, sub_len=sub_len,
                          scale=dh ** -0.5),
        grid=(dil, sub_len // tq),
        in_specs=[main(0), prev(1), main(1), nxt(1), prev(2), main(2), nxt(2)],
        out_specs=[pl.BlockSpec((None, tq, width), lambda r, i: (r, i, 0)),
                   pl.BlockSpec((None, tq, V7X_LANES), lambda r, i: (r, i, 0))],
        out_shape=[jax.ShapeDtypeStruct((dil, sub_len, width), F32),
                   jax.ShapeDtypeStruct((dil, sub_len, V7X_LANES), F32)],
        compiler_params=_params(("parallel", "arbitrary"), 6 * _nbytes((tq + 2 * BAND_HALO, width), qkv.dtype),
                                2 * _nbytes((tq, width), F32), 2 * _nbytes((tq, V7X_LANES), F32)),
        name=f"band_attention_d{dil}",
    )(qkv, qkv, qkv, qkv, qkv, qkv, qkv)


def _merge_kernel(*refs, dils, n_heads, dh):
    n_groups = len(dils)
    o_refs, l_refs, out_ref = refs[:n_groups], refs[n_groups:2 * n_groups], refs[2 * n_groups]
    o_nat, l_nat = refs[2 * n_groups + 1], refs[2 * n_groups + 2]
    tm = out_ref.shape[0]
    for g, dil in enumerate(dils):
        n = tm // dil
        for r in range(dil):
            rows = pl.ds(r, n, stride=dil) if dil > 1 else slice(0, tm)
            l_nat[g, rows, :] = l_refs[g][r]
            for h in range(n_heads):
                o_nat[g * n_heads + h, rows, :] = o_refs[g][r, :, h * dh:(h + 1) * dh]
    ls = [l_nat[g] for g in range(n_groups)]
    m = functools.reduce(jnp.maximum, ls)
    es = [jnp.exp(l - m) for l in ls]
    inv = 1.0 / functools.reduce(lambda a, b: a + b, es)
    ws = [e * inv for e in es]
    for h in range(n_heads):
        acc = ws[0][:, h:h + 1] * o_nat[h]
        for g in range(1, n_groups):
            acc = acc + ws[g][:, h:h + 1] * o_nat[g * n_heads + h]
        out_ref[:, h * dh:(h + 1) * dh] = acc.astype(out_ref.dtype)


def _merge_groups(outs, lses, dils, *, n_heads, dh, tm=256):
    width = n_heads * dh
    s = outs[0].shape[0] * outs[0].shape[1]
    tm = min(tm, s)
    n = len(outs)
    assert dh == V7X_LANES and all(tm % dil == 0 and (tm // dil) % V7X_SUBLANES == 0 for dil in dils)
    ospecs = [pl.BlockSpec((dil, tm // dil, width), lambda i: (0, i, 0)) for dil in dils]
    lspecs = [pl.BlockSpec((dil, tm // dil, V7X_LANES), lambda i: (0, i, 0)) for dil in dils]
    return pl.pallas_call(
        functools.partial(_merge_kernel, dils=tuple(dils), n_heads=n_heads, dh=dh),
        grid=(s // tm,),
        in_specs=ospecs + lspecs,
        out_specs=pl.BlockSpec((tm, width), lambda i: (i, 0)),
        out_shape=jax.ShapeDtypeStruct((s, width), BF16),
        scratch_shapes=[pltpu.VMEM((n * n_heads, tm, V7X_LANES), F32), pltpu.VMEM((n, tm, V7X_LANES), F32)],
        compiler_params=_params(("parallel",), 3 * n * _nbytes((tm, width), F32), 2 * _nbytes((tm, width), BF16),
                                3 * n * _nbytes((tm, V7X_LANES), F32)),
        name="merge_groups",
    )(*outs, *lses)


def _conv_ln_kernel(up_ref, u_ref, un_ref, w_ref, bdw_ref, g_ref, b_ref, o_ref, win_ref, y_ref, *,
                    ts, kw, rc, rl):
    i, j = pl.program_id(0), pl.program_id(1)
    n_i, n_j = pl.num_programs(0), pl.num_programs(1)
    cc = u_ref.shape[1]
    first = CONV_HALO - kw // 2
    win_ref[0:CONV_HALO, :] = jnp.where(i > 0, up_ref[...], 0.0)
    win_ref[CONV_HALO:CONV_HALO + ts, :] = u_ref[...]
    win_ref[CONV_HALO + ts:, :] = jnp.where(i < n_i - 1, un_ref[...], 0.0)
    for r0 in range(0, ts, rc):
        acc = jnp.broadcast_to(bdw_ref[...], (rc, cc))
        for k in range(kw):
            acc = acc + win_ref[r0 + first + k:r0 + first + k + rc, :] * w_ref[k:k + 1, :]
        y_ref[j, r0:r0 + rc, :] = acc

    @pl.when(j == n_j - 1)
    def _():
        n_c = y_ref.shape[0]
        inv_d = 1.0 / (n_c * cc)

        def ln_rows(rb, carry):
            r0 = pl.multiple_of(rb * rl, rl)
            ys = [y_ref[c, pl.ds(r0, rl), :] for c in range(n_c)]
            mu = functools.reduce(lambda a, b: a + b, [jnp.sum(y, axis=-1, keepdims=True) for y in ys]) * inv_d
            ycs = [y - mu for y in ys]
            var = functools.reduce(lambda a, b: a + b, [jnp.sum(y * y, axis=-1, keepdims=True) for y in ycs]) * inv_d
            inv = lax.rsqrt(var + EPS)
            for c in range(n_c):
                cols = slice(c * cc, (c + 1) * cc)
                z = ycs[c] * inv * g_ref[:, cols] + b_ref[:, cols]
                o_ref[pl.ds(r0, rl), cols] = (z * _sigmoid(z)).astype(o_ref.dtype)
            return carry

        lax.fori_loop(0, ts // rl, ln_rows, 0)


def _conv_ln_swish(u, w_dw, b_dw, ln_g, ln_b, *, ts=256, cc=512):
    s, d = u.shape
    kw = w_dw.shape[0]
    ts, cc = min(ts, s), min(cc, d)
    assert kw // 2 <= CONV_HALO and ts % CONV_HALO == 0 and s % ts == 0 and d % cc == 0
    hb = ts // CONV_HALO
    last = s // CONV_HALO - 1
    rc, rl = min(64, ts), min(16, ts)
    vec = pl.BlockSpec((1, d), lambda i, j: (0, 0))
    return pl.pallas_call(
        functools.partial(_conv_ln_kernel, ts=ts, kw=kw, rc=rc, rl=rl),
        grid=(s // ts, d // cc),
        in_specs=[pl.BlockSpec((CONV_HALO, cc), lambda i, j: (jnp.maximum(i * hb - 1, 0), j)),
                  pl.BlockSpec((ts, cc), lambda i, j: (i, j)),
                  pl.BlockSpec((CONV_HALO, cc), lambda i, j: (jnp.minimum((i + 1) * hb, last), j)),
                  pl.BlockSpec((kw, cc), lambda i, j: (0, j)),
                  pl.BlockSpec((1, cc), lambda i, j: (0, j)),
                  vec, vec],
        out_specs=pl.BlockSpec((ts, d), lambda i, j: (i, 0)),
        out_shape=jax.ShapeDtypeStruct((s, d), BF16),
        scratch_shapes=[pltpu.VMEM((ts + 2 * CONV_HALO, cc), F32), pltpu.VMEM((d // cc, ts, cc), F32)],
        compiler_params=_params(("parallel", "arbitrary"), 3 * _nbytes((ts + 2 * CONV_HALO, cc), F32),
                                2 * _nbytes((ts, d), BF16), _nbytes((ts, d), F32),
                                2 * _nbytes((kw + 3 * V7X_SUBLANES, d), F32)),
        name="conv_ln_swish",
    )(u, u, u, w_dw.astype(F32), b_dw.reshape(1, d).astype(F32), ln_g.reshape(1, d).astype(F32),
      ln_b.reshape(1, d).astype(F32))


def _mla_down_kernel(x_ref, wq_ref, wkv_ref, qn_ref, kvn_ref, c_ref, sa_ref, sb_ref, cq_ref, ckv_ref, kr_ref, *,
                     kv_rank, half):
    x = x_ref[...]
    a = jnp.dot(x, wq_ref[...], preferred_element_type=F32)
    a = a * lax.rsqrt(jnp.mean(a * a, axis=-1, keepdims=True) + EPS)
    cq_ref[...] = (a * qn_ref[...]).astype(cq_ref.dtype)
    b = jnp.dot(x, wkv_ref[...], preferred_element_type=F32)
    c = b[:, :kv_rank]
    c = c * lax.rsqrt(jnp.mean(c * c, axis=-1, keepdims=True) + EPS)
    ckv_ref[...] = (c * kvn_ref[...]).astype(ckv_ref.dtype)
    t = b[:, kv_rank:]
    kr_ref[...] = _rope_slab(t, c_ref[...], sa_ref[...], sb_ref[...], half).astype(kr_ref.dtype)


def _mla_down(x, w_dq, q_norm, w_dkv_pad, kv_norm, tables, kv_rank, *, tm=512):
    m, k = x.shape
    qr = w_dq.shape[1]
    nkv = w_dkv_pad.shape[1]
    tm = min(tm, m)
    tspec = pl.BlockSpec((None, tm, V7X_LANES), lambda i: (0, i, 0))
    return pl.pallas_call(
        functools.partial(_mla_down_kernel, kv_rank=kv_rank, half=C_ROPE // 2),
        grid=(m // tm,),
        in_specs=[pl.BlockSpec((tm, k), lambda i: (i, 0)),
                  pl.BlockSpec((k, qr), lambda i: (0, 0)),
                  pl.BlockSpec((k, nkv), lambda i: (0, 0)),
                  pl.BlockSpec((1, qr), lambda i: (0, 0)),
                  pl.BlockSpec((1, kv_rank), lambda i: (0, 0)),
                  tspec, tspec, tspec],
        out_specs=[pl.BlockSpec((tm, qr), lambda i: (i, 0)),
                   pl.BlockSpec((tm, kv_rank), lambda i: (i, 0)),
                   pl.BlockSpec((tm, V7X_LANES), lambda i: (i, 0))],
        out_shape=[jax.ShapeDtypeStruct((m, qr), BF16),
                   jax.ShapeDtypeStruct((m, kv_rank), BF16),
                   jax.ShapeDtypeStruct((m, V7X_LANES), BF16)],
        compiler_params=_params(("parallel",), 2 * _nbytes((tm, k), x.dtype), 2 * _nbytes((k, qr + nkv), BF16),
                                2 * _nbytes((tm, qr + nkv), BF16), 2 * _nbytes((tm, qr + nkv), F32),
                                6 * _nbytes((tm, V7X_LANES), F32)),
        name="mla_down",
    )(x, w_dq, w_dkv_pad, q_norm.reshape(1, qr).astype(F32), kv_norm.reshape(1, kv_rank).astype(F32), *tables)


def _mla_q_up_kernel(cq_ref, w_ref, c_ref, sa_ref, sb_ref, qt_ref, *, half, scale):
    r = jnp.dot(cq_ref[...], w_ref[...], preferred_element_type=F32)
    nope = r[:, :C_NOPE] * scale
    rope = _rope_slab(r[:, C_NOPE:], c_ref[...], sa_ref[...], sb_ref[...], half) * scale
    qt_ref[:C_NOPE, :] = nope.T.astype(qt_ref.dtype)
    qt_ref[C_NOPE:, :] = rope.T.astype(qt_ref.dtype)


def _mla_q_up(cq, w_uq_pad, tables, n_heads, scale, *, tm=1024):
    m, k = cq.shape
    hw = w_uq_pad.shape[1] // n_heads
    tm = min(tm, m)
    tspec = pl.BlockSpec((None, tm, V7X_LANES), lambda i, h: (0, i, 0))
    return pl.pallas_call(
        functools.partial(_mla_q_up_kernel, half=C_ROPE // 2, scale=scale),
        grid=(m // tm, n_heads),
        in_specs=[pl.BlockSpec((tm, k), lambda i, h: (i, 0)),
                  pl.BlockSpec((k, hw), lambda i, h: (0, h)),
                  tspec, tspec, tspec],
        out_specs=pl.BlockSpec((None, hw, tm), lambda i, h: (h, 0, i)),
        out_shape=jax.ShapeDtypeStruct((n_heads, hw, m), BF16),
        compiler_params=_params(("parallel", "arbitrary"), 2 * _nbytes((tm, k), BF16), 2 * _nbytes((k, hw), BF16),
                                2 * _nbytes((tm, hw), BF16), 3 * _nbytes((tm, hw), F32),
                                6 * _nbytes((tm, V7X_LANES), F32)),
        name="mla_q_up",
    )(cq, w_uq_pad, *tables)


def _mla_kv_up_kernel(c_ref, w_ref, kr_ref, k_ref, vt_ref):
    r = jnp.dot(c_ref[...], w_ref[...], preferred_element_type=F32)
    k_ref[:, :C_NOPE] = r[:, :C_NOPE].astype(k_ref.dtype)
    k_ref[:, C_NOPE:] = kr_ref[...]
    vt = r[:, C_NOPE:].T.astype(vt_ref.dtype)
    tk = vt_ref.shape[2]
    for c in range(vt_ref.shape[0]):
        vt_ref[c] = vt[:, c * tk:(c + 1) * tk]


def _mla_kv_up(c, w_ukv, kr, n_heads, tk, *, tm=1024):
    m, k = c.shape
    hw = w_ukv.shape[1] // n_heads
    tm = min(tm, m)
    assert tm % tk == 0 and m % tm == 0
    return pl.pallas_call(
        _mla_kv_up_kernel,
        grid=(m // tm, n_heads),
        in_specs=[pl.BlockSpec((tm, k), lambda i, h: (i, 0)),
                  pl.BlockSpec((k, hw), lambda i, h: (0, h)),
                  pl.BlockSpec((tm, V7X_LANES), lambda i, h: (i, 0))],
        out_specs=[pl.BlockSpec((None, tm, C_NOPE + V7X_LANES), lambda i, h: (h, i, 0)),
                   pl.BlockSpec((None, tm // tk, C_V, tk), lambda i, h: (h, i, 0, 0))],
        out_shape=[jax.ShapeDtypeStruct((n_heads, m, C_NOPE + V7X_LANES), BF16),
                   jax.ShapeDtypeStruct((n_heads, m // tk, C_V, tk), BF16)],
        compiler_params=_params(("parallel", "arbitrary"), 2 * _nbytes((tm, k), BF16), 2 * _nbytes((k, hw), BF16),
                                2 * _nbytes((tm, 3 * V7X_LANES), BF16), 3 * _nbytes((tm, hw), F32),
                                2 * _nbytes((tm, V7X_LANES), BF16)),
        name="mla_kv_up",
    )(c, w_ukv, kr)


def _flash_kernel(qt_ref, k_ref, vt_ref, o_ref, s_ref, m_ref, l_ref, acc_ref, *, tk, qs):
    n_kv = k_ref.shape[0] // tk
    tq = qt_ref.shape[1]
    m_ref[...] = jnp.full(m_ref.shape, NEG, F32)
    l_ref[...] = jnp.zeros(l_ref.shape, F32)
    acc_ref[...] = jnp.zeros(acc_ref.shape, F32)
    s_ref[...] = jnp.dot(k_ref[pl.ds(0, tk), :], qt_ref[...], preferred_element_type=F32)

    def step(kk, carry):
        nxt = jnp.minimum(kk + 1, n_kv - 1)
        kc = k_ref[pl.ds(pl.multiple_of(nxt * tk, tk), tk), :]
        vc = vt_ref[kk]
        for j in range(tq // qs):
            cols = slice(j * qs, (j + 1) * qs)
            s_next = jnp.dot(kc, qt_ref[:, cols], preferred_element_type=F32)
            st = s_ref[:, cols]
            m_old = m_ref[:, cols]
            m_new = jnp.maximum(m_old, jnp.max(st, axis=0, keepdims=True))
            alpha = jnp.exp2(m_old - m_new)
            pt = jnp.exp2(st - m_new)
            l_ref[:, cols] = alpha * l_ref[:, cols] + jnp.sum(pt, axis=0, keepdims=True)
            acc_ref[:, cols] = alpha * acc_ref[:, cols] + jnp.dot(vc, pt.astype(vc.dtype),
                                                                  preferred_element_type=F32)
            m_ref[:, cols] = m_new
            s_ref[:, cols] = s_next
        return carry

    lax.fori_loop(0, n_kv, step, 0)
    o = acc_ref[...] * (1.0 / l_ref[...])
    o_ref[...] = o.T.astype(o_ref.dtype)


def _flash_attention(qt, k, vt, *, tq=2048, qs=512):
    n_heads, dk, s = qt.shape
    n_chunks, dv, tk = vt.shape[1:]
    tq = min(tq, s)
    qs = min(qs, tq)
    assert n_chunks * tk == s and s % tq == 0 and tq % qs == 0
    return pl.pallas_call(
        functools.partial(_flash_kernel, tk=tk, qs=qs),
        grid=(n_heads, s // tq),
        in_specs=[pl.BlockSpec((None, dk, tq), lambda h, i: (h, 0, i)),
                  pl.BlockSpec((None, s, dk), lambda h, i: (h, 0, 0)),
                  pl.BlockSpec((None, n_chunks, dv, tk), lambda h, i: (h, 0, 0, 0))],
        out_specs=pl.BlockSpec((tq, dv), lambda h, i: (i, h)),
        out_shape=jax.ShapeDtypeStruct((s, n_heads * dv), BF16),
        scratch_shapes=[pltpu.VMEM((tk, tq), F32), pltpu.VMEM((1, tq), F32), pltpu.VMEM((1, tq), F32),
                        pltpu.VMEM((dv, tq), F32)],
        compiler_params=_params(("parallel", "arbitrary"), 2 * _nbytes((dk, tq), BF16),
                                2 * _nbytes((s, dk + dv), BF16), 2 * _nbytes((tq, dv), BF16),
                                _nbytes((tk + dv + 2 * V7X_SUBLANES, tq), F32), 2 * _nbytes((tk, qs), F32)),
        name="mla_flash_attention",
    )(qt, k, vt)


def _ffn(x, norm_g, w_gate, w_up, w_down):
    h = _rmsnorm(x, norm_g, BF16)
    a = _gated_matmul(h, w_gate.astype(BF16), w_up.astype(BF16), kind="swiglu", n_out=w_gate.shape[1],
                      name="ffn_gate_up")
    return _matmul(a, w_down.astype(BF16), residual=x, scale=0.5, name="ffn_down")


def _mixer_a(x, norm_g, w_qkv, w_o, tables):
    s, d = x.shape
    n_heads = d // (2 * A_HEAD_DIM)
    width = n_heads * A_HEAD_DIM
    dils = [dil for _, dil in A_GROUPS]
    hs = _rmsnorm_strided(x, norm_g, dils)
    w = w_qkv.astype(BF16)
    outs, lses = [], []
    for g, (window, dil) in enumerate(A_GROUPS):
        qkv = _qkv_rope(hs[g].reshape(s, d), w, tables[g], width, g)
        o, lse = _band_attention(qkv.reshape(dil, s // dil, 3 * width), dil, window // (2 * dil),
                                 n_heads=n_heads, dh=A_HEAD_DIM)
        outs.append(o)
        lses.append(lse)
    merged = _merge_groups(outs, lses, dils, n_heads=n_heads, dh=A_HEAD_DIM)
    return _matmul(merged, w_o.astype(BF16), residual=x, name="a_out_proj")


def _mixer_b(x, norm_g, w_pw1, b_pw1, w_dw, b_dw, ln_g, ln_b, w_pw2, b_pw2):
    d = x.shape[1]
    h = _rmsnorm(x, norm_g, BF16)
    w1 = w_pw1.astype(BF16)
    u = _gated_matmul(h, w1, w1, kind="glu", n_out=d, off2=d, b1=b_pw1, b2=b_pw1, out_dtype=F32, name="conv_pw1_glu")
    v = _conv_ln_swish(u, w_dw, b_dw, ln_g, ln_b)
    return _matmul(v, w_pw2.astype(BF16), bias=b_pw2, residual=x, name="conv_pw2")


def _mixer_c(x, norm_g, w_dq, q_norm, w_uq, w_dkv, kv_norm, w_ukv, w_o, tables):
    s, d = x.shape
    n_heads = d // C_V
    kv_rank = w_dkv.shape[1] - C_ROPE
    q_rank = w_dq.shape[1]
    pad = V7X_LANES - C_ROPE
    h = _rmsnorm(x, norm_g, BF16)
    w_dkv_pad = jnp.pad(w_dkv, ((0, 0), (0, pad))).astype(BF16)
    w_uq_pad = jnp.pad(w_uq.reshape(q_rank, n_heads, C_NOPE + C_ROPE), ((0, 0), (0, 0), (0, pad)))
    w_uq_pad = w_uq_pad.reshape(q_rank, n_heads * (C_NOPE + V7X_LANES)).astype(BF16)
    cq, ckv, kr = _mla_down(h, w_dq.astype(BF16), q_norm, w_dkv_pad, kv_norm, tables, kv_rank)
    qt = _mla_q_up(cq, w_uq_pad, tables, n_heads, (C_NOPE + C_ROPE) ** -0.5 * math.log2(math.e))
    k, vt = _mla_kv_up(ckv, w_ukv.astype(BF16), kr, n_heads, min(FLASH_KEY_CHUNK, s))
    o = _flash_attention(qt, k, vt)
    return _matmul(o, w_o.astype(BF16), residual=x, name="c_out_proj")


def kernel(x, ffn1_norm, ffn1_w_gate, ffn1_w_up, ffn1_w_down, mix_norm, ffn2_norm, ffn2_w_gate, ffn2_w_up,
           ffn2_w_down, final_norm, a_w_qkv, a_w_o, b_w_pw1, b_b_pw1, b_w_dw, b_b_dw, b_ln_g, b_ln_b, b_w_pw2,
           b_b_pw2, c_w_dq, c_q_norm, c_w_uq, c_w_dkv, c_kv_norm, c_w_ukv, c_w_o):
    batch, seq, d = x.shape
    depth = ffn1_norm.shape[0]
    tables_a = [_rope_tables(seq, A_ROPE_THETA, A_ROPE_DIMS, 2, dil) for _, dil in A_GROUPS]
    tables_c = _rope_tables(seq, C_ROPE_THETA, C_ROPE, 1) if depth >= N_MIXERS else None
    outs = []
    for b in range(batch):
        xb = x[b]
        for i in range(depth):
            j = i // N_MIXERS
            xb = _ffn(xb, ffn1_norm[i], ffn1_w_gate[i], ffn1_w_up[i], ffn1_w_down[i])
            if i % N_MIXERS == 0:
                xb = _mixer_a(xb, mix_norm[i], a_w_qkv[j], a_w_o[j], tables_a)
            elif i % N_MIXERS == 1:
                xb = _mixer_b(xb, mix_norm[i], b_w_pw1[j], b_b_pw1[j], b_w_dw[j], b_b_dw[j], b_ln_g[j], b_ln_b[j],
                              b_w_pw2[j], b_b_pw2[j])
            else:
                xb = _mixer_c(xb, mix_norm[i], c_w_dq[j], c_q_norm[j], c_w_uq[j], c_w_dkv[j], c_kv_norm[j],
                              c_w_ukv[j], c_w_o[j], tables_c)
            xb = _ffn(xb, ffn2_norm[i], ffn2_w_gate[i], ffn2_w_up[i], ffn2_w_down[i])
        outs.append(_rmsnorm(xb, final_norm, x.dtype))
    return jnp.stack(outs, axis=0)
```

```python
import functools
import math

import jax
import jax.numpy as jnp
from jax import lax
from jax.experimental import pallas as pl
from jax.experimental.pallas import tpu as pltpu

F32 = jnp.float32
BF16 = jnp.bfloat16

EPS = 1e-6
NEG = -1e30

N_MIXERS = 3
A_HEAD_DIM = 128
A_GROUPS = ((128, 1), (512, 4), (2048, 16))
A_ROPE_THETA = 500000.0
A_ROPE_DIMS = A_HEAD_DIM // 4
CONV_WIDTH = 31
C_NOPE = 128
C_ROPE = 64
C_V = 128
C_ROPE_THETA = 10000.0

V7X_VMEM_BYTES = 64 * 1024 * 1024
V7X_LANES = 128
V7X_SUBLANES = 8
COMPILER_SCRATCH_BYTES = 12 * 1024 * 1024

BAND_HALO = 64
BAND_SUB = 128
CONV_HALO = 16
FLASH_KEY_CHUNK = 512
FLASH_ONES_ROWS = 16


def _params(semantics, *buffer_bytes):
    need = sum(buffer_bytes) + COMPILER_SCRATCH_BYTES
    return pltpu.CompilerParams(dimension_semantics=semantics,
                                vmem_limit_bytes=min(need, V7X_VMEM_BYTES - 2 * 1024 * 1024))


def _nbytes(shape, dtype):
    return math.prod(shape) * jnp.dtype(dtype).itemsize


def _tile(n, target, align):
    if n <= target:
        return n
    t = (target // align) * align
    while n % t:
        t -= align
    return t


def _sigmoid(x):
    return 1.0 / (1.0 + jnp.exp(-x))


def _rmsnorm_kernel(x_ref, g_ref, o_ref):
    x = x_ref[...].astype(F32)
    y = x * lax.rsqrt(jnp.mean(x * x, axis=-1, keepdims=True) + EPS)
    o_ref[...] = (y * g_ref[...]).astype(o_ref.dtype)


def _rmsnorm(x, g, out_dtype, tm=256):
    s, d = x.shape
    tm = min(tm, s)
    return pl.pallas_call(
        _rmsnorm_kernel,
        grid=(s // tm,),
        in_specs=[pl.BlockSpec((tm, d), lambda i: (i, 0)),
                  pl.BlockSpec((1, d), lambda i: (0, 0))],
        out_specs=pl.BlockSpec((tm, d), lambda i: (i, 0)),
        out_shape=jax.ShapeDtypeStruct((s, d), out_dtype),
        compiler_params=_params(("parallel",), 2 * _nbytes((tm, d), x.dtype), 2 * _nbytes((tm, d), out_dtype),
                                2 * _nbytes((tm, d), F32)),
        name="rmsnorm",
    )(x, g.reshape(1, d).astype(F32))


def _cast_kernel(w_ref, o_ref):
    o_ref[...] = w_ref[...].astype(o_ref.dtype)


def _cast_bf16(stack, layer, block_bytes=4 * 1024 * 1024):
    _, rows, cols = stack.shape
    tr = _tile(rows, max(16, block_bytes // (cols * 4)), 16)
    return pl.pallas_call(
        _cast_kernel,
        grid=(rows // tr,),
        in_specs=[pl.BlockSpec((None, tr, cols), lambda r: (layer, r, 0))],
        out_specs=pl.BlockSpec((tr, cols), lambda r: (r, 0)),
        out_shape=jax.ShapeDtypeStruct((rows, cols), BF16),
        compiler_params=_params(("parallel",), 2 * _nbytes((tr, cols), F32), 2 * _nbytes((tr, cols), BF16)),
        name="cast_bf16",
    )(stack)


def _rmsnorm_strided_kernel(x_ref, g_ref, *refs, dils):
    n_perm = sum(1 for dil in dils if dil > 1)
    p_refs, o_refs = refs[:n_perm], refs[n_perm:]
    x = x_ref[...].astype(F32)
    y = x * lax.rsqrt(jnp.mean(x * x, axis=-1, keepdims=True) + EPS)
    y = (y * g_ref[...]).astype(BF16)
    tm = y.shape[0]
    p_iter = iter(p_refs)
    for o_ref, dil in zip(o_refs, dils):
        if dil == 1:
            o_ref[0] = y
            continue
        yp = jnp.dot(next(p_iter)[...], y, preferred_element_type=F32).astype(BF16)
        n = tm // dil
        for r in range(dil):
            o_ref[r] = yp[r * n:(r + 1) * n]


def _rmsnorm_strided(x, g, dils, tm=256):
    s, d = x.shape
    tm = min(tm, s)
    perms, pspecs, out_specs, out_shapes = [], [], [], []
    for dil in dils:
        n = tm // dil
        assert s % tm == 0 and (dil == 1 or (tm % dil == 0 and n % 16 == 0))
        if dil > 1:
            rows = jnp.arange(tm)
            src = (rows % n) * dil + rows // n
            perms.append((src[:, None] == jnp.arange(tm)[None, :]).astype(BF16))
            pspecs.append(pl.BlockSpec((tm, tm), lambda i: (0, 0)))
        out_specs.append(pl.BlockSpec((dil, n, d), lambda i: (0, i, 0)))
        out_shapes.append(jax.ShapeDtypeStruct((dil, s // dil, d), BF16))
    return pl.pallas_call(
        functools.partial(_rmsnorm_strided_kernel, dils=tuple(dils)),
        grid=(s // tm,),
        in_specs=[pl.BlockSpec((tm, d), lambda i: (i, 0)),
                  pl.BlockSpec((1, d), lambda i: (0, 0))] + pspecs,
        out_specs=out_specs,
        out_shape=out_shapes,
        compiler_params=_params(("parallel",), 2 * _nbytes((tm, d), x.dtype),
                                2 * len(dils) * _nbytes((tm, d), BF16), 3 * _nbytes((tm, d), F32)),
        name="rmsnorm_strided",
    )(x, g.reshape(1, d).astype(F32), *perms)


def _mm_kernel(*refs, has_bias, has_res, scale):
    x_ref, w_ref = refs[0], refs[1]
    pos = 2
    acc = jnp.dot(x_ref[...], w_ref[...], preferred_element_type=F32)
    if has_bias:
        acc = acc + refs[pos][...]
        pos += 1
    if has_res:
        acc = refs[pos][...] + scale * acc
        pos += 1
    o_ref = refs[pos]
    o_ref[...] = acc.astype(o_ref.dtype)


def _matmul(x, w, *, bias=None, residual=None, scale=1.0, out_dtype=F32, tm=1024, tn=512, name="matmul"):
    m, k = x.shape
    n = w.shape[1]
    tm, tn = _tile(m, tm, V7X_SUBLANES), _tile(n, tn, V7X_LANES)
    in_specs = [pl.BlockSpec((tm, k), lambda i, j: (i, 0)),
                pl.BlockSpec((k, tn), lambda i, j: (0, j))]
    args = [x, w]
    sizes = [2 * _nbytes((tm, k), x.dtype), 2 * _nbytes((k, tn), w.dtype), 2 * _nbytes((tm, tn), out_dtype),
             _nbytes((tm, tn), F32)]
    if bias is not None:
        in_specs.append(pl.BlockSpec((1, tn), lambda i, j: (0, j)))
        args.append(bias.reshape(1, n).astype(F32))
    if residual is not None:
        in_specs.append(pl.BlockSpec((tm, tn), lambda i, j: (i, j)))
        args.append(residual)
        sizes.append(2 * _nbytes((tm, tn), residual.dtype))
    return pl.pallas_call(
        functools.partial(_mm_kernel, has_bias=bias is not None, has_res=residual is not None, scale=scale),
        grid=(m // tm, n // tn),
        in_specs=in_specs,
        out_specs=pl.BlockSpec((tm, tn), lambda i, j: (i, j)),
        out_shape=jax.ShapeDtypeStruct((m, n), out_dtype),
        compiler_params=_params(("parallel", "arbitrary"), *sizes),
        name=name,
    )(*args)


def _gated_kernel(*refs, kind, has_bias):
    x_ref, w1_ref, w2_ref = refs[:3]
    x = x_ref[...]
    a = jnp.dot(x, w1_ref[...], preferred_element_type=F32)
    b = jnp.dot(x, w2_ref[...], preferred_element_type=F32)
    if has_bias:
        a = a + refs[3][...]
        b = b + refs[4][...]
    o_ref = refs[-1]
    if kind == "swiglu":
        out = (a * _sigmoid(a)) * b
    else:
        out = a * _sigmoid(b)
    o_ref[...] = out.astype(o_ref.dtype)


def _gated_matmul(x, w1, w2, *, kind, n_out, off2=0, b1=None, b2=None, out_dtype=BF16, tm=1024, tn=512,
                  name="gated"):
    m, k = x.shape
    tm, tn = _tile(m, tm, V7X_SUBLANES), _tile(n_out, tn, V7X_LANES)
    assert off2 % tn == 0
    ob = off2 // tn
    in_specs = [pl.BlockSpec((tm, k), lambda i, j: (i, 0)),
                pl.BlockSpec((k, tn), lambda i, j: (0, j)),
                pl.BlockSpec((k, tn), lambda i, j: (0, j + ob))]
    args = [x, w1, w2]
    has_bias = b1 is not None
    if has_bias:
        in_specs += [pl.BlockSpec((1, tn), lambda i, j: (0, j)),
                     pl.BlockSpec((1, tn), lambda i, j: (0, j + ob))]
        args += [b1.reshape(1, -1).astype(F32), b2.reshape(1, -1).astype(F32)]
    return pl.pallas_call(
        functools.partial(_gated_kernel, kind=kind, has_bias=has_bias),
        grid=(m // tm, n_out // tn),
        in_specs=in_specs,
        out_specs=pl.BlockSpec((tm, tn), lambda i, j: (i, j)),
        out_shape=jax.ShapeDtypeStruct((m, n_out), out_dtype),
        compiler_params=_params(("parallel", "arbitrary"), 2 * _nbytes((tm, k), x.dtype),
                                4 * _nbytes((k, tn), w1.dtype), 2 * _nbytes((tm, tn), out_dtype),
                                3 * _nbytes((tm, tn), F32)),
        name=name,
    )(*args)


def _rope_table_kernel(inv_ref, c_ref, sa_ref, sb_ref, *, tile, half, dil, sub_len):
    variant = pl.program_id(0)
    i = pl.program_id(1)
    shape = (tile, V7X_LANES)
    r = (i * tile) // sub_len
    l0 = (i * tile) % sub_len
    pos = ((l0 + lax.broadcasted_iota(jnp.int32, shape, 0)) * dil + r).astype(F32)
    lane = lax.broadcasted_iota(jnp.int32, shape, 1)
    ang = pos * inv_ref[...]
    cos, sin = jnp.cos(ang), jnp.sin(ang)
    rot = jnp.full(shape, variant, jnp.int32) == 0
    c_ref[...] = jnp.where(rot & (lane < 2 * half), cos, 1.0)
    sa_ref[...] = jnp.where(rot & (lane >= half) & (lane < 2 * half), sin, 0.0)
    sb_ref[...] = jnp.where(rot & (lane < half), -sin, 0.0)


def _rope_tables(seq, theta, n_rot, variants, dil=1, tile=512):
    half = n_rot // 2
    inv_freq = theta ** (-2.0 * jnp.arange(half, dtype=F32) / n_rot)
    inv = jnp.concatenate([inv_freq, inv_freq, jnp.zeros((V7X_LANES - n_rot,), F32)]).reshape(1, V7X_LANES)
    sub_len = seq // dil
    tile = min(tile, sub_len)
    assert sub_len % tile == 0
    spec = pl.BlockSpec((None, tile, V7X_LANES), lambda v, i: (v, i, 0))
    shape = jax.ShapeDtypeStruct((variants, seq, V7X_LANES), F32)
    return pl.pallas_call(
        functools.partial(_rope_table_kernel, tile=tile, half=half, dil=dil, sub_len=sub_len),
        grid=(variants, seq // tile),
        in_specs=[pl.BlockSpec((1, V7X_LANES), lambda v, i: (0, 0))],
        out_specs=[spec, spec, spec],
        out_shape=[shape, shape, shape],
        compiler_params=_params(("parallel", "parallel"), 12 * _nbytes((tile, V7X_LANES), F32)),
        name="rope_tables",
    )(inv)


def _rope_slab(t, c, sa, sb, half):
    return t * c + pltpu.roll(t, half, 1) * sa + pltpu.roll(t, V7X_LANES - half, 1) * sb


def _qkv_rope_kernel(x_ref, w_ref, c_ref, sa_ref, sb_ref, o_ref, *, half):
    acc = jnp.dot(x_ref[...], w_ref[...], preferred_element_type=F32)
    c, sa, sb = c_ref[...], sa_ref[...], sb_ref[...]
    for s in range(acc.shape[1] // V7X_LANES):
        cols = slice(s * V7X_LANES, (s + 1) * V7X_LANES)
        o_ref[:, cols] = _rope_slab(acc[:, cols], c, sa, sb, half).astype(o_ref.dtype)


def _qkv_rope(x, w, tables, width, group, *, tm=1024, tn=1024):
    m, k = x.shape
    n = 3 * width
    tm, tn = _tile(m, tm, V7X_SUBLANES), _tile(width, tn, V7X_LANES)
    per = width // tn
    col0 = group * 3 * per

    def tmap(i, j):
        return (jnp.where(j // per == 2, 1, 0), i, 0)

    tspec = pl.BlockSpec((None, tm, V7X_LANES), tmap)
    return pl.pallas_call(
        functools.partial(_qkv_rope_kernel, half=A_ROPE_DIMS // 2),
        grid=(m // tm, n // tn),
        in_specs=[pl.BlockSpec((tm, k), lambda i, j: (i, 0)),
                  pl.BlockSpec((k, tn), lambda i, j: (0, j + col0)),
                  tspec, tspec, tspec],
        out_specs=pl.BlockSpec((tm, tn), lambda i, j: (i, j)),
        out_shape=jax.ShapeDtypeStruct((m, n), BF16),
        compiler_params=_params(("parallel", "arbitrary"), 2 * _nbytes((tm, k), x.dtype),
                                2 * _nbytes((k, tn), w.dtype), 2 * _nbytes((tm, tn), BF16),
                                2 * _nbytes((tm, tn), F32), 6 * _nbytes((tm, V7X_LANES), F32)),
        name="qkv_rope",
    )(x, w, *tables)


def _band_kernel(q_ref, kp_ref, k_ref, kn_ref, vp_ref, v_ref, vn_ref, o_ref, lse_ref, *,
                 tq, n_heads, dh, half_w, sub_len, scale):
    i = pl.program_id(1)
    n_sub = tq // BAND_SUB
    win = BAND_SUB + 2 * BAND_HALO
    row = lax.broadcasted_iota(jnp.int32, (BAND_SUB, win), 0)
    col = lax.broadcasted_iota(jnp.int32, (BAND_SUB, win), 1)
    in_band = jnp.abs(col - BAND_HALO - row) <= half_w
    lane = lax.broadcasted_iota(jnp.int32, (BAND_SUB, V7X_LANES), 1)
    valid = []
    for j in range(n_sub):
        kpos = i * tq + (j * BAND_SUB - BAND_HALO) + col
        valid.append(in_band & (kpos >= 0) & (kpos < sub_len))
    lse_acc = [jnp.zeros((BAND_SUB, V7X_LANES), F32) for _ in range(n_sub)]
    for h in range(n_heads):
        hs = slice(h * dh, (h + 1) * dh)
        kc = jnp.concatenate([kp_ref[:, hs], k_ref[:, hs], kn_ref[:, hs]], axis=0)
        vc = jnp.concatenate([vp_ref[:, hs], v_ref[:, hs], vn_ref[:, hs]], axis=0)
        for j in range(n_sub):
            rows = slice(j * BAND_SUB, (j + 1) * BAND_SUB)
            q = q_ref[rows, hs]
            kw = kc[j * BAND_SUB:j * BAND_SUB + win]
            vw = vc[j * BAND_SUB:j * BAND_SUB + win]
            s = lax.dot_general(q, kw, (((1,), (1,)), ((), ())), preferred_element_type=F32) * scale
            s = jnp.where(valid[j], s, NEG)
            m = jnp.max(s, axis=-1, keepdims=True)
            e = jnp.exp(s - m)
            l = jnp.sum(e, axis=-1, keepdims=True)
            o = jnp.dot(e.astype(vw.dtype), vw, preferred_element_type=F32)
            o_ref[rows, hs] = o * (1.0 / l)
            lse_acc[j] = jnp.where(lane == h, m + jnp.log(l), lse_acc[j])
    for j in range(n_sub):
        lse_ref[j * BAND_SUB:(j + 1) * BAND_SUB, :] = lse_acc[j]


def _band_attention(qkv, dil, half_w, *, n_heads, dh, tq=256):
    dil_, sub_len, w = qkv.shape
    width = n_heads * dh
    tq = min(tq, sub_len)
    assert dil_ == dil and w == 3 * width
    assert sub_len % tq == 0 and tq % BAND_SUB == 0 and half_w <= BAND_HALO and n_heads <= V7X_LANES
    hb = tq // BAND_HALO
    last = sub_len // BAND_HALO - 1

    def main(sec):
        return pl.BlockSpec((None, tq, width), lambda r, i: (r, i, sec))

    def prev(sec):
        return pl.BlockSpec((None, BAND_HALO, width), lambda r, i: (r, jnp.maximum(i * hb - 1, 0), sec))

    def nxt(sec):
        return pl.BlockSpec((None, BAND_HALO, width), lambda r, i: (r, jnp.minimum((i + 1) * hb, last), sec))

    return pl.pallas_call(
        functools.partial(_band_kernel, tq=tq, n_heads=n_heads, dh=dh, half_w=half_w, sub_len=sub_len,
                          scale=dh ** -0.5),
        grid=(dil, sub_len // tq),
        in_specs=[main(0), prev(1), main(1), nxt(1), prev(2), main(2), nxt(2)],
        out_specs=[pl.BlockSpec((None, tq, width), lambda r, i: (r, i, 0)),
                   pl.BlockSpec((None, tq, V7X_LANES), lambda r, i: (r, i, 0))],
        out_shape=[jax.ShapeDtypeStruct((dil, sub_len, width), F32),
                   jax.ShapeDtypeStruct((dil, sub_len, V7X_LANES), F32)],
        compiler_params=_params(("parallel", "arbitrary"), 6 * _nbytes((tq + 2 * BAND_HALO, width), qkv.dtype),
                                2 * _nbytes((tq, width), F32), 2 * _nbytes((tq, V7X_LANES), F32)),
        name=f"band_attention_d{dil}",
    )(qkv, qkv, qkv, qkv, qkv, qkv, qkv)


def _merge_kernel(*refs, dils, n_heads, dh):
    n_groups = len(dils)
    o_refs, l_refs, out_ref = refs[:n_groups], refs[n_groups:2 * n_groups], refs[2 * n_groups]
    o_nat, l_nat = refs[2 * n_groups + 1], refs[2 * n_groups + 2]
    tm = out_ref.shape[0]
    for g, dil in enumerate(dils):
        n = tm // dil
        for r in range(dil):
            rows = pl.ds(r, n, stride=dil) if dil > 1 else slice(0, tm)
            l_nat[g, rows, :] = l_refs[g][r]
            for h in range(n_heads):
                o_nat[g * n_heads + h, rows, :] = o_refs[g][r, :, h * dh:(h + 1) * dh]
    ls = [l_nat[g] for g in range(n_groups)]
    m = functools.reduce(jnp.maximum, ls)
    es = [jnp.exp(l - m) for l in ls]
    inv = 1.0 / functools.reduce(lambda a, b: a + b, es)
    ws = [e * inv for e in es]
    for h in range(n_heads):
        acc = ws[0][:, h:h + 1] * o_nat[h]
        for g in range(1, n_groups):
            acc = acc + ws[g][:, h:h + 1] * o_nat[g * n_heads + h]
        out_ref[:, h * dh:(h + 1) * dh] = acc.astype(out_ref.dtype)


def _merge_groups(outs, lses, dils, *, n_heads, dh, tm=256):
    width = n_heads * dh
    s = outs[0].shape[0] * outs[0].shape[1]
    tm = min(tm, s)
    n = len(outs)
    assert dh == V7X_LANES and all(tm % dil == 0 and (tm // dil) % V7X_SUBLANES == 0 for dil in dils)
    ospecs = [pl.BlockSpec((dil, tm // dil, width), lambda i: (0, i, 0)) for dil in dils]
    lspecs = [pl.BlockSpec((dil, tm // dil, V7X_LANES), lambda i: (0, i, 0)) for dil in dils]
    return pl.pallas_call(
        functools.partial(_merge_kernel, dils=tuple(dils), n_heads=n_heads, dh=dh),
        grid=(s // tm,),
        in_specs=ospecs + lspecs,
        out_specs=pl.BlockSpec((tm, width), lambda i: (i, 0)),
        out_shape=jax.ShapeDtypeStruct((s, width), BF16),
        scratch_shapes=[pltpu.VMEM((n * n_heads, tm, V7X_LANES), F32), pltpu.VMEM((n, tm, V7X_LANES), F32)],
        compiler_params=_params(("parallel",), 3 * n * _nbytes((tm, width), F32), 2 * _nbytes((tm, width), BF16),
                                3 * n * _nbytes((tm, V7X_LANES), F32)),
        name="merge_groups",
    )(*outs, *lses)


def _conv_ln_kernel(up_ref, u_ref, un_ref, w_ref, bdw_ref, g_ref, b_ref, o_ref, win_ref, y_ref, *,
                    ts, kw, rc, rl):
    i, j = pl.program_id(0), pl.program_id(1)
    n_i, n_j = pl.num_programs(0), pl.num_programs(1)
    cc = u_ref.shape[1]
    first = CONV_HALO - kw // 2
    rows = ts + 2 * CONV_HALO
    win_ref[0, 0:CONV_HALO, :] = jnp.where(i > 0, up_ref[...], 0.0)
    win_ref[0, CONV_HALO:CONV_HALO + ts, :] = u_ref[...]
    win_ref[0, CONV_HALO + ts:, :] = jnp.where(i < n_i - 1, un_ref[...], 0.0)
    win = win_ref[0]
    for p in range(1, V7X_SUBLANES):
        win_ref[p] = pltpu.roll(win, rows - p, 0)
    for r0 in range(0, ts, rc):
        acc = jnp.broadcast_to(bdw_ref[...], (rc, cc))
        for k in range(kw):
            p = (first + k) % V7X_SUBLANES
            a = r0 + first + k - p
            wk = w_ref[k * V7X_SUBLANES:(k + 1) * V7X_SUBLANES, :]
            acc = acc + win_ref[p, a:a + rc, :] * pltpu.repeat(wk, rc // V7X_SUBLANES, 0)
        y_ref[j, r0:r0 + rc, :] = acc

    @pl.when(j == n_j - 1)
    def _():
        n_c = y_ref.shape[0]
        inv_d = 1.0 / (n_c * cc)

        def ln_rows(rb, carry):
            r0 = pl.multiple_of(rb * rl, rl)
            ys = [y_ref[c, pl.ds(r0, rl), :] for c in range(n_c)]
            mu = functools.reduce(lambda a, b: a + b, [jnp.sum(y, axis=-1, keepdims=True) for y in ys]) * inv_d
            ycs = [y - mu for y in ys]
            var = functools.reduce(lambda a, b: a + b, [jnp.sum(y * y, axis=-1, keepdims=True) for y in ycs]) * inv_d
            inv = lax.rsqrt(var + EPS)
            for c in range(n_c):
                cols = slice(c * cc, (c + 1) * cc)
                z = ycs[c] * inv * g_ref[:, cols] + b_ref[:, cols]
                o_ref[pl.ds(r0, rl), cols] = (z * _sigmoid(z)).astype(o_ref.dtype)
            return carry

        lax.fori_loop(0, ts // rl, ln_rows, 0)


def _conv_ln_swish(u, w_dw, b_dw, ln_g, ln_b, *, ts=256, cc=512):
    s, d = u.shape
    kw = w_dw.shape[0]
    ts, cc = min(ts, s), min(cc, d)
    assert kw // 2 <= CONV_HALO and ts % CONV_HALO == 0 and s % ts == 0 and d % cc == 0
    hb = ts // CONV_HALO
    last = s // CONV_HALO - 1
    rc, rl = min(64, ts), min(16, ts)
    vec = pl.BlockSpec((1, d), lambda i, j: (0, 0))
    return pl.pallas_call(
        functools.partial(_conv_ln_kernel, ts=ts, kw=kw, rc=rc, rl=rl),
        grid=(s // ts, d // cc),
        in_specs=[pl.BlockSpec((CONV_HALO, cc), lambda i, j: (jnp.maximum(i * hb - 1, 0), j)),
                  pl.BlockSpec((ts, cc), lambda i, j: (i, j)),
                  pl.BlockSpec((CONV_HALO, cc), lambda i, j: (jnp.minimum((i + 1) * hb, last), j)),
                  pl.BlockSpec((kw * V7X_SUBLANES, cc), lambda i, j: (0, j)),
                  pl.BlockSpec((1, cc), lambda i, j: (0, j)),
                  vec, vec],
        out_specs=pl.BlockSpec((ts, d), lambda i, j: (i, 0)),
        out_shape=jax.ShapeDtypeStruct((s, d), BF16),
        scratch_shapes=[pltpu.VMEM((V7X_SUBLANES, ts + 2 * CONV_HALO, cc), F32), pltpu.VMEM((d // cc, ts, cc), F32)],
        compiler_params=_params(("parallel", "arbitrary"), (2 + V7X_SUBLANES) * _nbytes((ts + 2 * CONV_HALO, cc), F32),
                                2 * _nbytes((ts, d), BF16), _nbytes((ts, d), F32),
                                2 * _nbytes((kw + 3 * V7X_SUBLANES, d), F32)),
        name="conv_ln_swish",
    )(u, u, u, jnp.repeat(w_dw.astype(F32), V7X_SUBLANES, axis=0), b_dw.reshape(1, d).astype(F32),
      ln_g.reshape(1, d).astype(F32), ln_b.reshape(1, d).astype(F32))


def _mla_down_kernel(x_ref, wq_ref, wkv_ref, qn_ref, kvn_ref, c_ref, sa_ref, sb_ref, cq_ref, ckv_ref, kr_ref, *,
                     kv_rank, half):
    x = x_ref[...]
    a = jnp.dot(x, wq_ref[...], preferred_element_type=F32)
    a = a * lax.rsqrt(jnp.mean(a * a, axis=-1, keepdims=True) + EPS)
    cq_ref[...] = (a * qn_ref[...]).astype(cq_ref.dtype)
    b = jnp.dot(x, wkv_ref[...], preferred_element_type=F32)
    c = b[:, :kv_rank]
    c = c * lax.rsqrt(jnp.mean(c * c, axis=-1, keepdims=True) + EPS)
    ckv_ref[...] = (c * kvn_ref[...]).astype(ckv_ref.dtype)
    t = b[:, kv_rank:]
    kr_ref[...] = _rope_slab(t, c_ref[...], sa_ref[...], sb_ref[...], half).astype(kr_ref.dtype)


def _mla_down(x, w_dq, q_norm, w_dkv_pad, kv_norm, tables, kv_rank, *, tm=512):
    m, k = x.shape
    qr = w_dq.shape[1]
    nkv = w_dkv_pad.shape[1]
    tm = min(tm, m)
    tspec = pl.BlockSpec((None, tm, V7X_LANES), lambda i: (0, i, 0))
    return pl.pallas_call(
        functools.partial(_mla_down_kernel, kv_rank=kv_rank, half=C_ROPE // 2),
        grid=(m // tm,),
        in_specs=[pl.BlockSpec((tm, k), lambda i: (i, 0)),
                  pl.BlockSpec((k, qr), lambda i: (0, 0)),
                  pl.BlockSpec((k, nkv), lambda i: (0, 0)),
                  pl.BlockSpec((1, qr), lambda i: (0, 0)),
                  pl.BlockSpec((1, kv_rank), lambda i: (0, 0)),
                  tspec, tspec, tspec],
        out_specs=[pl.BlockSpec((tm, qr), lambda i: (i, 0)),
                   pl.BlockSpec((tm, kv_rank), lambda i: (i, 0)),
                   pl.BlockSpec((tm, V7X_LANES), lambda i: (i, 0))],
        out_shape=[jax.ShapeDtypeStruct((m, qr), BF16),
                   jax.ShapeDtypeStruct((m, kv_rank), BF16),
                   jax.ShapeDtypeStruct((m, V7X_LANES), BF16)],
        compiler_params=_params(("parallel",), 2 * _nbytes((tm, k), x.dtype), 2 * _nbytes((k, qr + nkv), BF16),
                                2 * _nbytes((tm, qr + nkv), BF16), 2 * _nbytes((tm, qr + nkv), F32),
                                6 * _nbytes((tm, V7X_LANES), F32)),
        name="mla_down",
    )(x, w_dq, w_dkv_pad, q_norm.reshape(1, qr).astype(F32), kv_norm.reshape(1, kv_rank).astype(F32), *tables)


def _mla_q_up_kernel(cq_ref, w_ref, c_ref, sa_ref, sb_ref, qt_ref, *, half, scale):
    r = jnp.dot(cq_ref[...], w_ref[...], preferred_element_type=F32)
    hw = qt_ref.shape[1]
    for e in range(qt_ref.shape[0]):
        nope = r[:, e * hw:e * hw + C_NOPE] * scale
        rope = _rope_slab(r[:, e * hw + C_NOPE:(e + 1) * hw], c_ref[...], sa_ref[...], sb_ref[...], half) * scale
        qt_ref[e, :C_NOPE, :] = nope.T.astype(qt_ref.dtype)
        qt_ref[e, C_NOPE:, :] = rope.T.astype(qt_ref.dtype)


def _mla_q_up(cq, w_uq_pad, tables, n_heads, scale, *, tm=1024, heads_per_step=2):
    m, k = cq.shape
    hw = w_uq_pad.shape[1] // n_heads
    tm = min(tm, m)
    hp = heads_per_step if n_heads % heads_per_step == 0 else 1
    tspec = pl.BlockSpec((None, tm, V7X_LANES), lambda i, h: (0, i, 0))
    return pl.pallas_call(
        functools.partial(_mla_q_up_kernel, half=C_ROPE // 2, scale=scale),
        grid=(m // tm, n_heads // hp),
        in_specs=[pl.BlockSpec((tm, k), lambda i, h: (i, 0)),
                  pl.BlockSpec((k, hp * hw), lambda i, h: (0, h)),
                  tspec, tspec, tspec],
        out_specs=pl.BlockSpec((hp, hw, tm), lambda i, h: (h, 0, i)),
        out_shape=jax.ShapeDtypeStruct((n_heads, hw, m), BF16),
        compiler_params=_params(("parallel", "arbitrary"), 2 * _nbytes((tm, k), BF16),
                                2 * _nbytes((k, hp * hw), BF16), 2 * _nbytes((tm, hp * hw), BF16),
                                3 * _nbytes((tm, hp * hw), F32), 6 * _nbytes((tm, V7X_LANES), F32)),
        name="mla_q_up",
    )(cq, w_uq_pad, *tables)


def _mla_kv_up_kernel(c_ref, w_ref, kr_ref, k_ref, vt_ref):
    r = jnp.dot(c_ref[...], w_ref[...], preferred_element_type=F32)
    hw = C_NOPE + C_V
    n_chunks, _, tk = vt_ref.shape[1:]
    for e in range(k_ref.shape[0]):
        k_ref[e, :, :C_NOPE] = r[:, e * hw:e * hw + C_NOPE].astype(k_ref.dtype)
        k_ref[e, :, C_NOPE:] = kr_ref[...]
        vt = r[:, e * hw + C_NOPE:(e + 1) * hw].T.astype(vt_ref.dtype)
        for c in range(n_chunks):
            vt_ref[e, c, :C_V, :] = vt[:, c * tk:(c + 1) * tk]
            vt_ref[e, c, C_V:, :] = jnp.ones((FLASH_ONES_ROWS, tk), vt_ref.dtype)


def _mla_kv_up(c, w_ukv, kr, n_heads, tk, *, tm=1024, heads_per_step=2):
    m, k = c.shape
    hw = w_ukv.shape[1] // n_heads
    tm = min(tm, m)
    hp = heads_per_step if n_heads % heads_per_step == 0 else 1
    assert tm % tk == 0 and m % tm == 0 and hw == C_NOPE + C_V
    dva = C_V + FLASH_ONES_ROWS
    return pl.pallas_call(
        _mla_kv_up_kernel,
        grid=(m // tm, n_heads // hp),
        in_specs=[pl.BlockSpec((tm, k), lambda i, h: (i, 0)),
                  pl.BlockSpec((k, hp * hw), lambda i, h: (0, h)),
                  pl.BlockSpec((tm, V7X_LANES), lambda i, h: (i, 0))],
        out_specs=[pl.BlockSpec((hp, tm, C_NOPE + V7X_LANES), lambda i, h: (h, i, 0)),
                   pl.BlockSpec((hp, tm // tk, dva, tk), lambda i, h: (h, i, 0, 0))],
        out_shape=[jax.ShapeDtypeStruct((n_heads, m, C_NOPE + V7X_LANES), BF16),
                   jax.ShapeDtypeStruct((n_heads, m // tk, dva, tk), BF16)],
        compiler_params=_params(("parallel", "arbitrary"), 2 * _nbytes((tm, k), BF16),
                                2 * _nbytes((k, hp * hw), BF16), 2 * hp * _nbytes((tm, 3 * V7X_LANES), BF16),
                                3 * _nbytes((tm, hp * hw), F32), 2 * _nbytes((tm, V7X_LANES), BF16)),
        name="mla_kv_up",
    )(c, w_ukv, kr)


def _flash_kernel(qt_ref, k_ref, vt_ref, o_ref, s_ref, m_ref, acc_ref, *, tk, qs, dv):
    n_kv = k_ref.shape[0] // tk
    tq = qt_ref.shape[1]
    m_ref[...] = jnp.full(m_ref.shape, NEG, F32)
    acc_ref[...] = jnp.zeros(acc_ref.shape, F32)
    s_ref[...] = jnp.dot(k_ref[pl.ds(0, tk), :], qt_ref[...], preferred_element_type=F32)

    def step(kk, carry):
        nxt = jnp.minimum(kk + 1, n_kv - 1)
        kc = k_ref[pl.ds(pl.multiple_of(nxt * tk, tk), tk), :]
        vc = vt_ref[kk]
        for j in range(tq // qs):
            cols = slice(j * qs, (j + 1) * qs)
            s_next = jnp.dot(kc, qt_ref[:, cols], preferred_element_type=F32)
            st = s_ref[:, cols]
            m_old = m_ref[:, cols]
            m_new = jnp.maximum(m_old, jnp.max(st, axis=0, keepdims=True))
            alpha = jnp.exp2(m_old - m_new)
            pt = jnp.exp2(st - m_new)
            acc_ref[:, cols] = alpha * acc_ref[:, cols] + jnp.dot(vc, pt.astype(vc.dtype),
                                                                  preferred_element_type=F32)
            m_ref[:, cols] = m_new
            s_ref[:, cols] = s_next
        return carry

    lax.fori_loop(0, n_kv, step, 0)
    o = acc_ref[:dv, :] * (1.0 / acc_ref[dv:dv + 1, :])
    o_ref[...] = o.T.astype(o_ref.dtype)


def _flash_attention(qt, k, vt, *, tq=4096, qs=512):
    n_heads, dk, s = qt.shape
    n_chunks, dva, tk = vt.shape[1:]
    dv = dva - FLASH_ONES_ROWS
    tq = min(tq, s)
    qs = min(qs, tq)
    assert n_chunks * tk == s and s % tq == 0 and tq % qs == 0
    return pl.pallas_call(
        functools.partial(_flash_kernel, tk=tk, qs=qs, dv=dv),
        grid=(n_heads, s // tq),
        in_specs=[pl.BlockSpec((None, dk, tq), lambda h, i: (h, 0, i)),
                  pl.BlockSpec((None, s, dk), lambda h, i: (h, 0, 0)),
                  pl.BlockSpec((None, n_chunks, dva, tk), lambda h, i: (h, 0, 0, 0))],
        out_specs=pl.BlockSpec((tq, dv), lambda h, i: (i, h)),
        out_shape=jax.ShapeDtypeStruct((s, n_heads * dv), BF16),
        scratch_shapes=[pltpu.VMEM((tk, tq), F32), pltpu.VMEM((1, tq), F32), pltpu.VMEM((dva, tq), F32)],
        compiler_params=_params(("parallel", "arbitrary"), 2 * _nbytes((dk, tq), BF16),
                                2 * _nbytes((s, dk + dva), BF16), 2 * _nbytes((tq, dv), BF16),
                                _nbytes((tk + dva + V7X_SUBLANES, tq), F32), 2 * _nbytes((tk, qs), F32)),
        name="mla_flash_attention",
    )(qt, k, vt)


def _ffn(x, norm_g, w_gate, w_up, w_down, layer):
    h = _rmsnorm(x, norm_g, BF16)
    a = _gated_matmul(h, _cast_bf16(w_gate, layer), _cast_bf16(w_up, layer), kind="swiglu", n_out=w_gate.shape[2],
                      name="ffn_gate_up")
    return _matmul(a, _cast_bf16(w_down, layer), residual=x, scale=0.5, name="ffn_down")


def _mixer_a(x, norm_g, w_qkv, w_o, layer, tables):
    s, d = x.shape
    n_heads = d // (2 * A_HEAD_DIM)
    width = n_heads * A_HEAD_DIM
    dils = [dil for _, dil in A_GROUPS]
    hs = _rmsnorm_strided(x, norm_g, dils)
    w = _cast_bf16(w_qkv, layer)
    outs, lses = [], []
    for g, (window, dil) in enumerate(A_GROUPS):
        qkv = _qkv_rope(hs[g].reshape(s, d), w, tables[g], width, g)
        o, lse = _band_attention(qkv.reshape(dil, s // dil, 3 * width), dil, window // (2 * dil),
                                 n_heads=n_heads, dh=A_HEAD_DIM)
        outs.append(o)
        lses.append(lse)
    merged = _merge_groups(outs, lses, dils, n_heads=n_heads, dh=A_HEAD_DIM)
    return _matmul(merged, _cast_bf16(w_o, layer), residual=x, name="a_out_proj")


def _mixer_b(x, norm_g, w_pw1, b_pw1, w_dw, b_dw, ln_g, ln_b, w_pw2, b_pw2, layer):
    d = x.shape[1]
    h = _rmsnorm(x, norm_g, BF16)
    w1 = _cast_bf16(w_pw1, layer)
    u = _gated_matmul(h, w1, w1, kind="glu", n_out=d, off2=d, b1=b_pw1[layer], b2=b_pw1[layer], out_dtype=F32,
                      name="conv_pw1_glu")
    v = _conv_ln_swish(u, w_dw[layer], b_dw[layer], ln_g[layer], ln_b[layer])
    return _matmul(v, _cast_bf16(w_pw2, layer), bias=b_pw2[layer], residual=x, name="conv_pw2")


def _mixer_c(x, norm_g, w_dq, q_norm, w_uq, w_dkv, kv_norm, w_ukv, w_o, layer, tables):
    s, d = x.shape
    n_heads = d // C_V
    kv_rank = w_dkv.shape[2] - C_ROPE
    q_rank = w_dq.shape[2]
    pad = V7X_LANES - C_ROPE
    h = _rmsnorm(x, norm_g, BF16)
    w_dkv_pad = jnp.pad(w_dkv[layer], ((0, 0), (0, pad))).astype(BF16)
    w_uq_pad = jnp.pad(w_uq[layer].reshape(q_rank, n_heads, C_NOPE + C_ROPE), ((0, 0), (0, 0), (0, pad)))
    w_uq_pad = w_uq_pad.reshape(q_rank, n_heads * (C_NOPE + V7X_LANES)).astype(BF16)
    cq, ckv, kr = _mla_down(h, _cast_bf16(w_dq, layer), q_norm[layer], w_dkv_pad, kv_norm[layer], tables, kv_rank)
    qt = _mla_q_up(cq, w_uq_pad, tables, n_heads, (C_NOPE + C_ROPE) ** -0.5 * math.log2(math.e))
    k, vt = _mla_kv_up(ckv, _cast_bf16(w_ukv, layer), kr, n_heads, min(FLASH_KEY_CHUNK, s))
    o = _flash_attention(qt, k, vt)
    return _matmul(o, _cast_bf16(w_o, layer), residual=x, name="c_out_proj")


def kernel(x, ffn1_norm, ffn1_w_gate, ffn1_w_up, ffn1_w_down, mix_norm, ffn2_norm, ffn2_w_gate, ffn2_w_up,
           ffn2_w_down, final_norm, a_w_qkv, a_w_o, b_w_pw1, b_b_pw1, b_w_dw, b_b_dw, b_ln_g, b_ln_b, b_w_pw2,
           b_b_pw2, c_w_dq, c_q_norm, c_w_uq, c_w_dkv, c_kv_norm, c_w_ukv, c_w_o):
    batch, seq, d = x.shape
    depth = ffn1_norm.shape[0]
    tables_a = [_rope_tables(seq, A_ROPE_THETA, A_ROPE_DIMS, 2, dil) for _, dil in A_GROUPS]
    tables_c = _rope_tables(seq, C_ROPE_THETA, C_ROPE, 1) if depth >= N_MIXERS else None
    outs = []
    for b in range(batch):
        xb = x[b]
        for i in range(depth):
            j = i // N_MIXERS
            xb = _ffn(xb, ffn1_norm[i], ffn1_w_gate, ffn1_w_up, ffn1_w_down, i)
            if i % N_MIXERS == 0:
                xb = _mixer_a(xb, mix_norm[i], a_w_qkv, a_w_o, j, tables_a)
            elif i % N_MIXERS == 1:
                xb = _mixer_b(xb, mix_norm[i], b_w_pw1, b_b_pw1, b_w_dw, b_b_dw, b_ln_g, b_ln_b, b_w_pw2, b_b_pw2, j)
            else:
                xb = _mixer_c(xb, mix_norm[i], c_w_dq, c_q_norm, c_w_uq, c_w_dkv, c_kv_norm, c_w_ukv, c_w_o, j,
                              tables_c)
            xb = _ffn(xb, ffn2_norm[i], ffn2_w_gate, ffn2_w_up, ffn2_w_down, i)
        outs.append(_rmsnorm(xb, final_norm, x.dtype))
    return jnp.stack(outs, axis=0)
```

```python
import functools
import math

import jax
import jax.numpy as jnp
from jax import lax
from jax.experimental import pallas as pl
from jax.experimental.pallas import tpu as pltpu

F32 = jnp.float32
BF16 = jnp.bfloat16

EPS = 1e-6
NEG = -1e30

N_MIXERS = 3
A_HEAD_DIM = 128
A_GROUPS = ((128, 1), (512, 4), (2048, 16))
A_ROPE_THETA = 500000.0
A_ROPE_DIMS = A_HEAD_DIM // 4
CONV_WIDTH = 31
C_NOPE = 128
C_ROPE = 64
C_V = 128
C_ROPE_THETA = 10000.0

V7X_VMEM_BYTES = 64 * 1024 * 1024
V7X_LANES = 128
V7X_SUBLANES = 8
COMPILER_SCRATCH_BYTES = 12 * 1024 * 1024

BAND_HALO = 64
BAND_SUB = 128
CONV_HALO = 16
FLASH_KEY_CHUNK = 512
FLASH_ONES_ROWS = 16


def _params(semantics, *buffer_bytes):
    need = sum(buffer_bytes) + COMPILER_SCRATCH_BYTES
    return pltpu.CompilerParams(dimension_semantics=semantics,
                                vmem_limit_bytes=min(need, V7X_VMEM_BYTES - 2 * 1024 * 1024))


def _nbytes(shape, dtype):
    return math.prod(shape) * jnp.dtype(dtype).itemsize


def _tile(n, target, align):
    if n <= target:
        return n
    t = (target // align) * align
    while n % t:
        t -= align
    return t


def _sigmoid(x):
    return 1.0 / (1.0 + jnp.exp(-x))


def _rmsnorm_kernel(x_ref, g_ref, o_ref):
    x = x_ref[...].astype(F32)
    y = x * lax.rsqrt(jnp.mean(x * x, axis=-1, keepdims=True) + EPS)
    o_ref[...] = (y * g_ref[...]).astype(o_ref.dtype)


def _rmsnorm(x, g, out_dtype, tm=256):
    s, d = x.shape
    tm = min(tm, s)
    return pl.pallas_call(
        _rmsnorm_kernel,
        grid=(s // tm,),
        in_specs=[pl.BlockSpec((tm, d), lambda i: (i, 0)),
                  pl.BlockSpec((1, d), lambda i: (0, 0))],
        out_specs=pl.BlockSpec((tm, d), lambda i: (i, 0)),
        out_shape=jax.ShapeDtypeStruct((s, d), out_dtype),
        compiler_params=_params(("parallel",), 2 * _nbytes((tm, d), x.dtype), 2 * _nbytes((tm, d), out_dtype),
                                2 * _nbytes((tm, d), F32)),
        name="rmsnorm",
    )(x, g.reshape(1, d).astype(F32))


def _prenorm_kernel(x_ref, g_ref, xg_ref, ss_ref):
    x = x_ref[...].astype(F32)
    xg_ref[...] = (x * g_ref[...]).astype(xg_ref.dtype)
    ss_ref[...] = jnp.broadcast_to(jnp.sum(x * x, axis=-1, keepdims=True), ss_ref.shape)


def _prenorm(x, g, tm=256):
    s, d = x.shape
    tm = min(tm, s)
    return pl.pallas_call(
        _prenorm_kernel,
        grid=(s // tm,),
        in_specs=[pl.BlockSpec((tm, d), lambda i: (i, 0)),
                  pl.BlockSpec((1, d), lambda i: (0, 0))],
        out_specs=[pl.BlockSpec((tm, d), lambda i: (i, 0)),
                   pl.BlockSpec((tm, V7X_LANES), lambda i: (i, 0))],
        out_shape=[jax.ShapeDtypeStruct((s, d), BF16), jax.ShapeDtypeStruct((s, V7X_LANES), F32)],
        compiler_params=_params(("parallel",), 2 * _nbytes((tm, d), x.dtype), 2 * _nbytes((tm, d), BF16),
                                2 * _nbytes((tm, d), F32)),
        name="prenorm",
    )(x, g.reshape(1, d).astype(F32))


def _cast_kernel(w_ref, o_ref):
    o_ref[...] = w_ref[...].astype(o_ref.dtype)


def _cast_bf16(stack, layer, block_bytes=4 * 1024 * 1024):
    _, rows, cols = stack.shape
    tr = _tile(rows, max(16, block_bytes // (cols * 4)), 16)
    return pl.pallas_call(
        _cast_kernel,
        grid=(rows // tr,),
        in_specs=[pl.BlockSpec((None, tr, cols), lambda r: (layer, r, 0))],
        out_specs=pl.BlockSpec((tr, cols), lambda r: (r, 0)),
        out_shape=jax.ShapeDtypeStruct((rows, cols), BF16),
        compiler_params=_params(("parallel",), 2 * _nbytes((tr, cols), F32), 2 * _nbytes((tr, cols), BF16)),
        name="cast_bf16",
    )(stack)


def _rmsnorm_strided_kernel(x_ref, g_ref, *refs, dils):
    n_perm = sum(1 for dil in dils if dil > 1)
    p_refs, o_refs = refs[:n_perm], refs[n_perm:]
    x = x_ref[...].astype(F32)
    y = x * lax.rsqrt(jnp.mean(x * x, axis=-1, keepdims=True) + EPS)
    y = (y * g_ref[...]).astype(BF16)
    tm = y.shape[0]
    p_iter = iter(p_refs)
    for o_ref, dil in zip(o_refs, dils):
        if dil == 1:
            o_ref[0] = y
            continue
        yp = jnp.dot(next(p_iter)[...], y, preferred_element_type=F32).astype(BF16)
        n = tm // dil
        for r in range(dil):
            o_ref[r] = yp[r * n:(r + 1) * n]


def _rmsnorm_strided(x, g, dils, tm=256):
    s, d = x.shape
    tm = min(tm, s)
    perms, pspecs, out_specs, out_shapes = [], [], [], []
    for dil in dils:
        n = tm // dil
        assert s % tm == 0 and (dil == 1 or (tm % dil == 0 and n % 16 == 0))
        if dil > 1:
            rows = jnp.arange(tm)
            src = (rows % n) * dil + rows // n
            perms.append((src[:, None] == jnp.arange(tm)[None, :]).astype(BF16))
            pspecs.append(pl.BlockSpec((tm, tm), lambda i: (0, 0)))
        out_specs.append(pl.BlockSpec((dil, n, d), lambda i: (0, i, 0)))
        out_shapes.append(jax.ShapeDtypeStruct((dil, s // dil, d), BF16))
    return pl.pallas_call(
        functools.partial(_rmsnorm_strided_kernel, dils=tuple(dils)),
        grid=(s // tm,),
        in_specs=[pl.BlockSpec((tm, d), lambda i: (i, 0)),
                  pl.BlockSpec((1, d), lambda i: (0, 0))] + pspecs,
        out_specs=out_specs,
        out_shape=out_shapes,
        compiler_params=_params(("parallel",), 2 * _nbytes((tm, d), x.dtype),
                                2 * len(dils) * _nbytes((tm, d), BF16), 3 * _nbytes((tm, d), F32)),
        name="rmsnorm_strided",
    )(x, g.reshape(1, d).astype(F32), *perms)


def _mm_kernel(*refs, has_bias, has_res, has_next, scale):
    x_ref, w_ref = refs[0], refs[1]
    pos = 2
    acc = jnp.dot(x_ref[...], w_ref[...], preferred_element_type=F32)
    if has_bias:
        acc = acc + refs[pos][...]
        pos += 1
    if has_res:
        acc = refs[pos][...] + scale * acc
        pos += 1
    if has_next:
        gn_ref = refs[pos]
        pos += 1
    o_ref = refs[pos]
    o_ref[...] = acc.astype(o_ref.dtype)
    if has_next:
        xg_ref, ss_ref = refs[pos + 1], refs[pos + 2]
        xg_ref[...] = (acc * gn_ref[...]).astype(xg_ref.dtype)
        part = jnp.broadcast_to(jnp.sum(acc * acc, axis=-1, keepdims=True), ss_ref.shape)
        j = pl.program_id(1)

        @pl.when(j == 0)
        def _():
            ss_ref[...] = part

        @pl.when(j > 0)
        def _():
            ss_ref[...] = ss_ref[...] + part


def _matmul(x, w, *, bias=None, residual=None, scale=1.0, next_gain=None, out_dtype=F32, tm=1024, tn=512,
            name="matmul"):
    m, k = x.shape
    n = w.shape[1]
    tm, tn = _tile(m, tm, V7X_SUBLANES), _tile(n, tn, V7X_LANES)
    in_specs = [pl.BlockSpec((tm, k), lambda i, j: (i, 0)),
                pl.BlockSpec((k, tn), lambda i, j: (0, j))]
    args = [x, w]
    sizes = [2 * _nbytes((tm, k), x.dtype), 2 * _nbytes((k, tn), w.dtype), 2 * _nbytes((tm, tn), out_dtype),
             _nbytes((tm, tn), F32)]
    if bias is not None:
        in_specs.append(pl.BlockSpec((1, tn), lambda i, j: (0, j)))
        args.append(bias.reshape(1, n).astype(F32))
    if residual is not None:
        in_specs.append(pl.BlockSpec((tm, tn), lambda i, j: (i, j)))
        args.append(residual)
        sizes.append(2 * _nbytes((tm, tn), residual.dtype))
    out_specs = pl.BlockSpec((tm, tn), lambda i, j: (i, j))
    out_shape = jax.ShapeDtypeStruct((m, n), out_dtype)
    if next_gain is not None:
        in_specs.append(pl.BlockSpec((1, tn), lambda i, j: (0, j)))
        args.append(next_gain.reshape(1, n).astype(F32))
        out_specs = [out_specs, pl.BlockSpec((tm, tn), lambda i, j: (i, j)),
                     pl.BlockSpec((tm, V7X_LANES), lambda i, j: (i, 0))]
        out_shape = [out_shape, jax.ShapeDtypeStruct((m, n), BF16), jax.ShapeDtypeStruct((m, V7X_LANES), F32)]
        sizes += [2 * _nbytes((tm, tn), BF16), 2 * _nbytes((tm, V7X_LANES), F32), _nbytes((tm, tn), F32)]
    return pl.pallas_call(
        functools.partial(_mm_kernel, has_bias=bias is not None, has_res=residual is not None,
                          has_next=next_gain is not None, scale=scale),
        grid=(m // tm, n // tn),
        in_specs=in_specs,
        out_specs=out_specs,
        out_shape=out_shape,
        compiler_params=_params(("parallel", "arbitrary"), *sizes),
        name=name,
    )(*args)


def _row_rsqrt(ss_ref, width):
    return lax.rsqrt(ss_ref[:, :1] * (1.0 / width) + EPS)


def _gated_kernel(*refs, kind, has_bias):
    x_ref, ss_ref, w1_ref, w2_ref = refs[:4]
    x = x_ref[...]
    r = _row_rsqrt(ss_ref, x.shape[1])
    a = jnp.dot(x, w1_ref[...], preferred_element_type=F32) * r
    b = jnp.dot(x, w2_ref[...], preferred_element_type=F32) * r
    if has_bias:
        a = a + refs[4][...]
        b = b + refs[5][...]
    o_ref = refs[-1]
    if kind == "swiglu":
        out = (a * _sigmoid(a)) * b
    else:
        out = a * _sigmoid(b)
    o_ref[...] = out.astype(o_ref.dtype)


def _gated_matmul(x, ss, w1, w2, *, kind, n_out, off2=0, b1=None, b2=None, out_dtype=BF16, tm=1024, tn=512,
                  name="gated"):
    m, k = x.shape
    tm, tn = _tile(m, tm, V7X_SUBLANES), _tile(n_out, tn, V7X_LANES)
    assert off2 % tn == 0
    ob = off2 // tn
    in_specs = [pl.BlockSpec((tm, k), lambda i, j: (i, 0)),
                pl.BlockSpec((tm, V7X_LANES), lambda i, j: (i, 0)),
                pl.BlockSpec((k, tn), lambda i, j: (0, j)),
                pl.BlockSpec((k, tn), lambda i, j: (0, j + ob))]
    args = [x, ss, w1, w2]
    has_bias = b1 is not None
    if has_bias:
        in_specs += [pl.BlockSpec((1, tn), lambda i, j: (0, j)),
                     pl.BlockSpec((1, tn), lambda i, j: (0, j + ob))]
        args += [b1.reshape(1, -1).astype(F32), b2.reshape(1, -1).astype(F32)]
    return pl.pallas_call(
        functools.partial(_gated_kernel, kind=kind, has_bias=has_bias),
        grid=(m // tm, n_out // tn),
        in_specs=in_specs,
        out_specs=pl.BlockSpec((tm, tn), lambda i, j: (i, j)),
        out_shape=jax.ShapeDtypeStruct((m, n_out), out_dtype),
        compiler_params=_params(("parallel", "arbitrary"), 2 * _nbytes((tm, k), x.dtype),
                                4 * _nbytes((k, tn), w1.dtype), 2 * _nbytes((tm, tn), out_dtype),
                                3 * _nbytes((tm, tn), F32)),
        name=name,
    )(*args)


def _rope_table_kernel(inv_ref, c_ref, sa_ref, sb_ref, *, tile, half, dil, sub_len):
    variant = pl.program_id(0)
    i = pl.program_id(1)
    shape = (tile, V7X_LANES)
    r = (i * tile) // sub_len
    l0 = (i * tile) % sub_len
    pos = ((l0 + lax.broadcasted_iota(jnp.int32, shape, 0)) * dil + r).astype(F32)
    lane = lax.broadcasted_iota(jnp.int32, shape, 1)
    ang = pos * inv_ref[...]
    cos, sin = jnp.cos(ang), jnp.sin(ang)
    rot = jnp.full(shape, variant, jnp.int32) == 0
    c_ref[...] = jnp.where(rot & (lane < 2 * half), cos, 1.0)
    sa_ref[...] = jnp.where(rot & (lane >= half) & (lane < 2 * half), sin, 0.0)
    sb_ref[...] = jnp.where(rot & (lane < half), -sin, 0.0)


def _rope_tables(seq, theta, n_rot, variants, dil=1, tile=512):
    half = n_rot // 2
    inv_freq = theta ** (-2.0 * jnp.arange(half, dtype=F32) / n_rot)
    inv = jnp.concatenate([inv_freq, inv_freq, jnp.zeros((V7X_LANES - n_rot,), F32)]).reshape(1, V7X_LANES)
    sub_len = seq // dil
    tile = min(tile, sub_len)
    assert sub_len % tile == 0
    spec = pl.BlockSpec((None, tile, V7X_LANES), lambda v, i: (v, i, 0))
    shape = jax.ShapeDtypeStruct((variants, seq, V7X_LANES), F32)
    return pl.pallas_call(
        functools.partial(_rope_table_kernel, tile=tile, half=half, dil=dil, sub_len=sub_len),
        grid=(variants, seq // tile),
        in_specs=[pl.BlockSpec((1, V7X_LANES), lambda v, i: (0, 0))],
        out_specs=[spec, spec, spec],
        out_shape=[shape, shape, shape],
        compiler_params=_params(("parallel", "parallel"), 12 * _nbytes((tile, V7X_LANES), F32)),
        name="rope_tables",
    )(inv)


def _rope_slab(t, c, sa, sb, half):
    return t * c + pltpu.roll(t, half, 1) * sa + pltpu.roll(t, V7X_LANES - half, 1) * sb


def _qkv_rope_kernel(x_ref, w_ref, c_ref, sa_ref, sb_ref, o_ref, *, half):
    acc = jnp.dot(x_ref[...], w_ref[...], preferred_element_type=F32)
    c, sa, sb = c_ref[...], sa_ref[...], sb_ref[...]
    for s in range(acc.shape[1] // V7X_LANES):
        cols = slice(s * V7X_LANES, (s + 1) * V7X_LANES)
        o_ref[:, cols] = _rope_slab(acc[:, cols], c, sa, sb, half).astype(o_ref.dtype)


def _qkv_rope(x, w, tables, width, group, *, tm=1024, tn=1024):
    m, k = x.shape
    n = 3 * width
    tm, tn = _tile(m, tm, V7X_SUBLANES), _tile(width, tn, V7X_LANES)
    per = width // tn
    col0 = group * 3 * per

    def tmap(i, j):
        return (jnp.where(j // per == 2, 1, 0), i, 0)

    tspec = pl.BlockSpec((None, tm, V7X_LANES), tmap)
    return pl.pallas_call(
        functools.partial(_qkv_rope_kernel, half=A_ROPE_DIMS // 2),
        grid=(m // tm, n // tn),
        in_specs=[pl.BlockSpec((tm, k), lambda i, j: (i, 0)),
                  pl.BlockSpec((k, tn), lambda i, j: (0, j + col0)),
                  tspec, tspec, tspec],
        out_specs=pl.BlockSpec((tm, tn), lambda i, j: (i, j)),
        out_shape=jax.ShapeDtypeStruct((m, n), BF16),
        compiler_params=_params(("parallel", "arbitrary"), 2 * _nbytes((tm, k), x.dtype),
                                2 * _nbytes((k, tn), w.dtype), 2 * _nbytes((tm, tn), BF16),
                                2 * _nbytes((tm, tn), F32), 6 * _nbytes((tm, V7X_LANES), F32)),
        name="qkv_rope",
    )(x, w, *tables)


def _band_kernel(q_ref, kp_ref, k_ref, kn_ref, vp_ref, v_ref, vn_ref, o_ref, lse_ref, *,
                 tq, n_heads, dh, half_w, sub_len, scale):
    i = pl.program_id(1)
    n_sub = tq // BAND_SUB
    win = BAND_SUB + 2 * BAND_HALO
    row = lax.broadcasted_iota(jnp.int32, (BAND_SUB, win), 0)
    col = lax.broadcasted_iota(jnp.int32, (BAND_SUB, win), 1)
    in_band = jnp.abs(col - BAND_HALO - row) <= half_w
    lane = lax.broadcasted_iota(jnp.int32, (BAND_SUB, V7X_LANES), 1)
    valid = []
    for j in range(n_sub):
        kpos = i * tq + (j * BAND_SUB - BAND_HALO) + col
        valid.append(in_band & (kpos >= 0) & (kpos < sub_len))
    lse_acc = [jnp.zeros((BAND_SUB, V7X_LANES), F32) for _ in range(n_sub)]
    for h in range(n_heads):
        hs = slice(h * dh, (h + 1) * dh)
        kc = jnp.concatenate([kp_ref[:, hs], k_ref[:, hs], kn_ref[:, hs]], axis=0)
        vc = jnp.concatenate([vp_ref[:, hs], v_ref[:, hs], vn_ref[:, hs]], axis=0)
        for j in range(n_sub):
            rows = slice(j * BAND_SUB, (j + 1) * BAND_SUB)
            q = q_ref[rows, hs]
            kw = kc[j * BAND_SUB:j * BAND_SUB + win]
            vw = vc[j * BAND_SUB:j * BAND_SUB + win]
            s = lax.dot_general(q, kw, (((1,), (1,)), ((), ())), preferred_element_type=F32) * scale
            s = jnp.where(valid[j], s, NEG)
            m = jnp.max(s, axis=-1, keepdims=True)
            e = jnp.exp(s - m)
            l = jnp.sum(e, axis=-1, keepdims=True)
            o = jnp.dot(e.astype(vw.dtype), vw, preferred_element_type=F32)
            o_ref[rows, hs] = o * (1.0 / l)
            lse_acc[j] = jnp.where(lane == h, m + jnp.log(l), lse_acc[j])
    for j in range(n_sub):
        lse_ref[j * BAND_SUB:(j + 1) * BAND_SUB, :] = lse_acc[j]


def _band_attention(qkv, dil, half_w, *, n_heads, dh, tq=256):
    dil_, sub_len, w = qkv.shape
    width = n_heads * dh
    tq = min(tq, sub_len)
    assert dil_ == dil and w == 3 * width
    assert sub_len % tq == 0 and tq % BAND_SUB == 0 and half_w <= BAND_HALO and n_heads <= V7X_LANES
    hb = tq // BAND_HALO
    last = sub_len // BAND_HALO - 1

    def main(sec):
        return pl.BlockSpec((None, tq, width), lambda r, i: (r, i, sec))

    def prev(sec):
        return pl.BlockSpec((None, BAND_HALO, width), lambda r, i: (r, jnp.maximum(i * hb - 1, 0), sec))

    def nxt(sec):
        return pl.BlockSpec((None, BAND_HALO, width), lambda r, i: (r, jnp.minimum((i + 1) * hb, last), sec))

    return pl.pallas_call(
        functools.partial(_band_kernel, tq=tq, n_heads=n_heads, dh=dh, half_w=half_w, sub_len=sub_len,
                          scale=dh ** -0.5),
        grid=(dil, sub_len // tq),
        in_specs=[main(0), prev(1), main(1), nxt(1), prev(2), main(2), nxt(2)],
        out_specs=[pl.BlockSpec((None, tq, width), lambda r, i: (r, i, 0)),
                   pl.BlockSpec((None, tq, V7X_LANES), lambda r, i: (r, i, 0))],
        out_shape=[jax.ShapeDtypeStruct((dil, sub_len, width), F32),
                   jax.ShapeDtypeStruct((dil, sub_len, V7X_LANES), F32)],
        compiler_params=_params(("parallel", "arbitrary"), 6 * _nbytes((tq + 2 * BAND_HALO, width), qkv.dtype),
                                2 * _nbytes((tq, width), F32), 2 * _nbytes((tq, V7X_LANES), F32)),
        name=f"band_attention_d{dil}",
    )(qkv, qkv, qkv, qkv, qkv, qkv, qkv)


def _merge_kernel(*refs, dils, n_heads, dh):
    n_groups = len(dils)
    o_refs, l_refs, out_ref = refs[:n_groups], refs[n_groups:2 * n_groups], refs[2 * n_groups]
    o_nat, l_nat = refs[2 * n_groups + 1], refs[2 * n_groups + 2]
    tm = out_ref.shape[0]
    for g, dil in enumerate(dils):
        n = tm // dil
        for r in range(dil):
            rows = pl.ds(r, n, stride=dil) if dil > 1 else slice(0, tm)
            l_nat[g, rows, :] = l_refs[g][r]
            for h in range(n_heads):
                o_nat[g * n_heads + h, rows, :] = o_refs[g][r, :, h * dh:(h + 1) * dh]
    ls = [l_nat[g] for g in range(n_groups)]
    m = functools.reduce(jnp.maximum, ls)
    es = [jnp.exp(l - m) for l in ls]
    inv = 1.0 / functools.reduce(lambda a, b: a + b, es)
    ws = [e * inv for e in es]
    for h in range(n_heads):
        acc = ws[0][:, h:h + 1] * o_nat[h]
        for g in range(1, n_groups):
            acc = acc + ws[g][:, h:h + 1] * o_nat[g * n_heads + h]
        out_ref[:, h * dh:(h + 1) * dh] = acc.astype(out_ref.dtype)


def _merge_groups(outs, lses, dils, *, n_heads, dh, tm=256):
    width = n_heads * dh
    s = outs[0].shape[0] * outs[0].shape[1]
    tm = min(tm, s)
    n = len(outs)
    assert dh == V7X_LANES and all(tm % dil == 0 and (tm // dil) % V7X_SUBLANES == 0 for dil in dils)
    ospecs = [pl.BlockSpec((dil, tm // dil, width), lambda i: (0, i, 0)) for dil in dils]
    lspecs = [pl.BlockSpec((dil, tm // dil, V7X_LANES), lambda i: (0, i, 0)) for dil in dils]
    return pl.pallas_call(
        functools.partial(_merge_kernel, dils=tuple(dils), n_heads=n_heads, dh=dh),
        grid=(s // tm,),
        in_specs=ospecs + lspecs,
        out_specs=pl.BlockSpec((tm, width), lambda i: (i, 0)),
        out_shape=jax.ShapeDtypeStruct((s, width), BF16),
        scratch_shapes=[pltpu.VMEM((n * n_heads, tm, V7X_LANES), F32), pltpu.VMEM((n, tm, V7X_LANES), F32)],
        compiler_params=_params(("parallel",), 3 * n * _nbytes((tm, width), F32), 2 * _nbytes((tm, width), BF16),
                                3 * n * _nbytes((tm, V7X_LANES), F32)),
        name="merge_groups",
    )(*outs, *lses)


def _conv_ln_kernel(up_ref, u_ref, un_ref, w_ref, bdw_ref, g_ref, b_ref, o_ref, win_ref, y_ref, *,
                    ts, kw, rc, rl):
    i, j = pl.program_id(0), pl.program_id(1)
    n_i, n_j = pl.num_programs(0), pl.num_programs(1)
    cc = u_ref.shape[1]
    first = CONV_HALO - kw // 2
    rows = ts + 2 * CONV_HALO
    win_ref[0, 0:CONV_HALO, :] = jnp.where(i > 0, up_ref[...], 0.0)
    win_ref[0, CONV_HALO:CONV_HALO + ts, :] = u_ref[...]
    win_ref[0, CONV_HALO + ts:, :] = jnp.where(i < n_i - 1, un_ref[...], 0.0)
    win = win_ref[0]
    for p in range(1, V7X_SUBLANES):
        win_ref[p] = pltpu.roll(win, rows - p, 0)
    groups = rc // V7X_SUBLANES
    for r0 in range(0, ts, rc):
        acc = jnp.broadcast_to(bdw_ref[...], (groups, V7X_SUBLANES, cc))
        for k in range(kw):
            p = (first + k) % V7X_SUBLANES
            a = r0 + first + k - p
            wk = w_ref[k * V7X_SUBLANES:(k + 1) * V7X_SUBLANES, :]
            acc = acc + win_ref[p, a:a + rc, :].reshape(groups, V7X_SUBLANES, cc) * wk[None]
        y_ref[j, r0:r0 + rc, :] = acc.reshape(rc, cc)

    @pl.when(j == n_j - 1)
    def _():
        n_c = y_ref.shape[0]
        inv_d = 1.0 / (n_c * cc)

        def ln_rows(rb, carry):
            r0 = pl.multiple_of(rb * rl, rl)
            ys = [y_ref[c, pl.ds(r0, rl), :] for c in range(n_c)]
            mu = functools.reduce(lambda a, b: a + b, [jnp.sum(y, axis=-1, keepdims=True) for y in ys]) * inv_d
            ycs = [y - mu for y in ys]
            var = functools.reduce(lambda a, b: a + b, [jnp.sum(y * y, axis=-1, keepdims=True) for y in ycs]) * inv_d
            inv = lax.rsqrt(var + EPS)
            for c in range(n_c):
                cols = slice(c * cc, (c + 1) * cc)
                z = ycs[c] * inv * g_ref[:, cols] + b_ref[:, cols]
                o_ref[pl.ds(r0, rl), cols] = (z * _sigmoid(z)).astype(o_ref.dtype)
            return carry

        lax.fori_loop(0, ts // rl, ln_rows, 0)


def _conv_ln_swish(u, w_dw, b_dw, ln_g, ln_b, *, ts=256, cc=512):
    s, d = u.shape
    kw = w_dw.shape[0]
    ts, cc = min(ts, s), min(cc, d)
    assert kw // 2 <= CONV_HALO and ts % CONV_HALO == 0 and s % ts == 0 and d % cc == 0
    hb = ts // CONV_HALO
    last = s // CONV_HALO - 1
    rc, rl = min(64, ts), min(16, ts)
    vec = pl.BlockSpec((1, d), lambda i, j: (0, 0))
    return pl.pallas_call(
        functools.partial(_conv_ln_kernel, ts=ts, kw=kw, rc=rc, rl=rl),
        grid=(s // ts, d // cc),
        in_specs=[pl.BlockSpec((CONV_HALO, cc), lambda i, j: (jnp.maximum(i * hb - 1, 0), j)),
                  pl.BlockSpec((ts, cc), lambda i, j: (i, j)),
                  pl.BlockSpec((CONV_HALO, cc), lambda i, j: (jnp.minimum((i + 1) * hb, last), j)),
                  pl.BlockSpec((kw * V7X_SUBLANES, cc), lambda i, j: (0, j)),
                  pl.BlockSpec((1, cc), lambda i, j: (0, j)),
                  vec, vec],
        out_specs=pl.BlockSpec((ts, d), lambda i, j: (i, 0)),
        out_shape=jax.ShapeDtypeStruct((s, d), BF16),
        scratch_shapes=[pltpu.VMEM((V7X_SUBLANES, ts + 2 * CONV_HALO, cc), F32), pltpu.VMEM((d // cc, ts, cc), F32)],
        compiler_params=_params(("parallel", "arbitrary"), (2 + V7X_SUBLANES) * _nbytes((ts + 2 * CONV_HALO, cc), F32),
                                2 * _nbytes((ts, d), BF16), _nbytes((ts, d), F32),
                                2 * _nbytes((kw + 3 * V7X_SUBLANES, d), F32)),
        name="conv_ln_swish",
    )(u, u, u, jnp.repeat(w_dw.astype(F32), V7X_SUBLANES, axis=0), b_dw.reshape(1, d).astype(F32),
      ln_g.reshape(1, d).astype(F32), ln_b.reshape(1, d).astype(F32))


def _mla_down_kernel(x_ref, ss_ref, wq_ref, wkv_ref, qn_ref, kvn_ref, c_ref, sa_ref, sb_ref, cq_ref, ckv_ref, kr_ref,
                     *, kv_rank, half):
    x = x_ref[...]
    r = _row_rsqrt(ss_ref, x.shape[1])
    a = jnp.dot(x, wq_ref[...], preferred_element_type=F32) * r
    a = a * lax.rsqrt(jnp.mean(a * a, axis=-1, keepdims=True) + EPS)
    cq_ref[...] = (a * qn_ref[...]).astype(cq_ref.dtype)
    b = jnp.dot(x, wkv_ref[...], preferred_element_type=F32) * r
    c = b[:, :kv_rank]
    c = c * lax.rsqrt(jnp.mean(c * c, axis=-1, keepdims=True) + EPS)
    ckv_ref[...] = (c * kvn_ref[...]).astype(ckv_ref.dtype)
    t = b[:, kv_rank:]
    kr_ref[...] = _rope_slab(t, c_ref[...], sa_ref[...], sb_ref[...], half).astype(kr_ref.dtype)


def _mla_down(x, ss, w_dq, q_norm, w_dkv_pad, kv_norm, tables, kv_rank, *, tm=512):
    m, k = x.shape
    qr = w_dq.shape[1]
    nkv = w_dkv_pad.shape[1]
    tm = min(tm, m)
    tspec = pl.BlockSpec((None, tm, V7X_LANES), lambda i: (0, i, 0))
    return pl.pallas_call(
        functools.partial(_mla_down_kernel, kv_rank=kv_rank, half=C_ROPE // 2),
        grid=(m // tm,),
        in_specs=[pl.BlockSpec((tm, k), lambda i: (i, 0)),
                  pl.BlockSpec((tm, V7X_LANES), lambda i: (i, 0)),
                  pl.BlockSpec((k, qr), lambda i: (0, 0)),
                  pl.BlockSpec((k, nkv), lambda i: (0, 0)),
                  pl.BlockSpec((1, qr), lambda i: (0, 0)),
                  pl.BlockSpec((1, kv_rank), lambda i: (0, 0)),
                  tspec, tspec, tspec],
        out_specs=[pl.BlockSpec((tm, qr), lambda i: (i, 0)),
                   pl.BlockSpec((tm, kv_rank), lambda i: (i, 0)),
                   pl.BlockSpec((tm, V7X_LANES), lambda i: (i, 0))],
        out_shape=[jax.ShapeDtypeStruct((m, qr), BF16),
                   jax.ShapeDtypeStruct((m, kv_rank), BF16),
                   jax.ShapeDtypeStruct((m, V7X_LANES), BF16)],
        compiler_params=_params(("parallel",), 2 * _nbytes((tm, k), x.dtype), 2 * _nbytes((k, qr + nkv), BF16),
                                2 * _nbytes((tm, qr + nkv), BF16), 2 * _nbytes((tm, qr + nkv), F32),
                                6 * _nbytes((tm, V7X_LANES), F32)),
        name="mla_down",
    )(x, ss, w_dq, w_dkv_pad, q_norm.reshape(1, qr).astype(F32), kv_norm.reshape(1, kv_rank).astype(F32), *tables)


def _mla_q_up_kernel(cq_ref, w_ref, c_ref, sa_ref, sb_ref, qt_ref, *, half, scale):
    r = jnp.dot(cq_ref[...], w_ref[...], preferred_element_type=F32)
    hw = qt_ref.shape[1]
    for e in range(qt_ref.shape[0]):
        nope = r[:, e * hw:e * hw + C_NOPE] * scale
        rope = _rope_slab(r[:, e * hw + C_NOPE:(e + 1) * hw], c_ref[...], sa_ref[...], sb_ref[...], half) * scale
        qt_ref[e, :C_NOPE, :] = nope.T.astype(qt_ref.dtype)
        qt_ref[e, C_NOPE:, :] = rope.T.astype(qt_ref.dtype)


def _mla_q_up(cq, w_uq_pad, tables, n_heads, scale, *, tm=1024, heads_per_step=2):
    m, k = cq.shape
    hw = w_uq_pad.shape[1] // n_heads
    tm = min(tm, m)
    hp = heads_per_step if n_heads % heads_per_step == 0 else 1
    tspec = pl.BlockSpec((None, tm, V7X_LANES), lambda i, h: (0, i, 0))
    return pl.pallas_call(
        functools.partial(_mla_q_up_kernel, half=C_ROPE // 2, scale=scale),
        grid=(m // tm, n_heads // hp),
        in_specs=[pl.BlockSpec((tm, k), lambda i, h: (i, 0)),
                  pl.BlockSpec((k, hp * hw), lambda i, h: (0, h)),
                  tspec, tspec, tspec],
        out_specs=pl.BlockSpec((hp, hw, tm), lambda i, h: (h, 0, i)),
        out_shape=jax.ShapeDtypeStruct((n_heads, hw, m), BF16),
        compiler_params=_params(("parallel", "arbitrary"), 2 * _nbytes((tm, k), BF16),
                                2 * _nbytes((k, hp * hw), BF16), 2 * _nbytes((tm, hp * hw), BF16),
                                3 * _nbytes((tm, hp * hw), F32), 6 * _nbytes((tm, V7X_LANES), F32)),
        name="mla_q_up",
    )(cq, w_uq_pad, *tables)


def _mla_kv_up_kernel(c_ref, w_ref, kr_ref, k_ref, vt_ref):
    r = jnp.dot(c_ref[...], w_ref[...], preferred_element_type=F32)
    hw = C_NOPE + C_V
    n_chunks, _, tk = vt_ref.shape[1:]
    for e in range(k_ref.shape[0]):
        k_ref[e, :, :C_NOPE] = r[:, e * hw:e * hw + C_NOPE].astype(k_ref.dtype)
        k_ref[e, :, C_NOPE:] = kr_ref[...]
        vt = r[:, e * hw + C_NOPE:(e + 1) * hw].T.astype(vt_ref.dtype)
        for c in range(n_chunks):
            vt_ref[e, c, :C_V, :] = vt[:, c * tk:(c + 1) * tk]
            vt_ref[e, c, C_V:, :] = jnp.ones((FLASH_ONES_ROWS, tk), vt_ref.dtype)


def _mla_kv_up(c, w_ukv, kr, n_heads, tk, *, tm=1024, heads_per_step=2):
    m, k = c.shape
    hw = w_ukv.shape[1] // n_heads
    tm = min(tm, m)
    hp = heads_per_step if n_heads % heads_per_step == 0 else 1
    assert tm % tk == 0 and m % tm == 0 and hw == C_NOPE + C_V
    dva = C_V + FLASH_ONES_ROWS
    return pl.pallas_call(
        _mla_kv_up_kernel,
        grid=(m // tm, n_heads // hp),
        in_specs=[pl.BlockSpec((tm, k), lambda i, h: (i, 0)),
                  pl.BlockSpec((k, hp * hw), lambda i, h: (0, h)),
                  pl.BlockSpec((tm, V7X_LANES), lambda i, h: (i, 0))],
        out_specs=[pl.BlockSpec((hp, tm, C_NOPE + V7X_LANES), lambda i, h: (h, i, 0)),
                   pl.BlockSpec((hp, tm // tk, dva, tk), lambda i, h: (h, i, 0, 0))],
        out_shape=[jax.ShapeDtypeStruct((n_heads, m, C_NOPE + V7X_LANES), BF16),
                   jax.ShapeDtypeStruct((n_heads, m // tk, dva, tk), BF16)],
        compiler_params=_params(("parallel", "arbitrary"), 2 * _nbytes((tm, k), BF16),
                                2 * _nbytes((k, hp * hw), BF16), 2 * hp * _nbytes((tm, 3 * V7X_LANES), BF16),
                                3 * _nbytes((tm, hp * hw), F32), 2 * _nbytes((tm, V7X_LANES), BF16)),
        name="mla_kv_up",
    )(c, w_ukv, kr)


def _flash_kernel(qt_ref, k_ref, vt_ref, o_ref, s_ref, m_ref, acc_ref, *, tk, qs, dv):
    n_kv = k_ref.shape[0] // tk
    tq = qt_ref.shape[1]
    m_ref[...] = jnp.full(m_ref.shape, NEG, F32)
    acc_ref[...] = jnp.zeros(acc_ref.shape, F32)
    s_ref[...] = jnp.dot(k_ref[pl.ds(0, tk), :], qt_ref[...], preferred_element_type=F32)

    def step(kk, carry):
        nxt = jnp.minimum(kk + 1, n_kv - 1)
        kc = k_ref[pl.ds(pl.multiple_of(nxt * tk, tk), tk), :]
        vc = vt_ref[kk]
        for j in range(tq // qs):
            cols = slice(j * qs, (j + 1) * qs)
            s_next = jnp.dot(kc, qt_ref[:, cols], preferred_element_type=F32)
            st = s_ref[:, cols]
            m_old = m_ref[:, cols]
            m_new = jnp.maximum(m_old, jnp.max(st, axis=0, keepdims=True))
            alpha = jnp.exp2(m_old - m_new)
            pt = jnp.exp2(st - m_new)
            acc_ref[:, cols] = alpha * acc_ref[:, cols] + jnp.dot(vc, pt.astype(vc.dtype),
                                                                  preferred_element_type=F32)
            m_ref[:, cols] = m_new
            s_ref[:, cols] = s_next
        return carry

    lax.fori_loop(0, n_kv, step, 0)
    o = acc_ref[:dv, :] * (1.0 / acc_ref[dv:dv + 1, :])
    o_ref[...] = o.T.astype(o_ref.dtype)


def _flash_attention(qt, k, vt, *, tq=4096, qs=512):
    n_heads, dk, s = qt.shape
    n_chunks, dva, tk = vt.shape[1:]
    dv = dva - FLASH_ONES_ROWS
    tq = min(tq, s)
    qs = min(qs, tq)
    assert n_chunks * tk == s and s % tq == 0 and tq % qs == 0
    return pl.pallas_call(
        functools.partial(_flash_kernel, tk=tk, qs=qs, dv=dv),
        grid=(n_heads, s // tq),
        in_specs=[pl.BlockSpec((None, dk, tq), lambda h, i: (h, 0, i)),
                  pl.BlockSpec((None, s, dk), lambda h, i: (h, 0, 0)),
                  pl.BlockSpec((None, n_chunks, dva, tk), lambda h, i: (h, 0, 0, 0))],
        out_specs=pl.BlockSpec((tq, dv), lambda h, i: (i, h)),
        out_shape=jax.ShapeDtypeStruct((s, n_heads * dv), BF16),
        scratch_shapes=[pltpu.VMEM((tk, tq), F32), pltpu.VMEM((1, tq), F32), pltpu.VMEM((dva, tq), F32)],
        compiler_params=_params(("parallel", "arbitrary"), 2 * _nbytes((dk, tq), BF16),
                                2 * _nbytes((s, dk + dva), BF16), 2 * _nbytes((tq, dv), BF16),
                                _nbytes((tk + dva + V7X_SUBLANES, tq), F32), 2 * _nbytes((tk, qs), F32)),
        name="mla_flash_attention",
    )(qt, k, vt)


def _with_pre(res):
    if isinstance(res, (list, tuple)):
        return res[0], (res[1], res[2])
    return res, None


def _ffn(x, pre, w_gate, w_up, w_down, layer, next_gain):
    a = _gated_matmul(pre[0], pre[1], _cast_bf16(w_gate, layer), _cast_bf16(w_up, layer), kind="swiglu",
                      n_out=w_gate.shape[2], name="ffn_gate_up")
    return _with_pre(_matmul(a, _cast_bf16(w_down, layer), residual=x, scale=0.5, next_gain=next_gain,
                             name="ffn_down"))


def _mixer_a(x, norm_g, w_qkv, w_o, layer, tables, next_gain):
    s, d = x.shape
    n_heads = d // (2 * A_HEAD_DIM)
    width = n_heads * A_HEAD_DIM
    dils = [dil for _, dil in A_GROUPS]
    hs = _rmsnorm_strided(x, norm_g, dils)
    w = _cast_bf16(w_qkv, layer)
    outs, lses = [], []
    for g, (window, dil) in enumerate(A_GROUPS):
        qkv = _qkv_rope(hs[g].reshape(s, d), w, tables[g], width, g)
        o, lse = _band_attention(qkv.reshape(dil, s // dil, 3 * width), dil, window // (2 * dil),
                                 n_heads=n_heads, dh=A_HEAD_DIM)
        outs.append(o)
        lses.append(lse)
    merged = _merge_groups(outs, lses, dils, n_heads=n_heads, dh=A_HEAD_DIM)
    return _with_pre(_matmul(merged, _cast_bf16(w_o, layer), residual=x, next_gain=next_gain, name="a_out_proj"))


def _mixer_b(x, pre, w_pw1, b_pw1, w_dw, b_dw, ln_g, ln_b, w_pw2, b_pw2, layer, next_gain):
    d = x.shape[1]
    w1 = _cast_bf16(w_pw1, layer)
    u = _gated_matmul(pre[0], pre[1], w1, w1, kind="glu", n_out=d, off2=d, b1=b_pw1[layer], b2=b_pw1[layer],
                      out_dtype=F32, name="conv_pw1_glu")
    v = _conv_ln_swish(u, w_dw[layer], b_dw[layer], ln_g[layer], ln_b[layer])
    return _with_pre(_matmul(v, _cast_bf16(w_pw2, layer), bias=b_pw2[layer], residual=x, next_gain=next_gain,
                             name="conv_pw2"))


def _mixer_c(x, pre, w_dq, q_norm, w_uq, w_dkv, kv_norm, w_ukv, w_o, layer, tables, next_gain):
    s, d = x.shape
    n_heads = d // C_V
    kv_rank = w_dkv.shape[2] - C_ROPE
    q_rank = w_dq.shape[2]
    pad = V7X_LANES - C_ROPE
    w_dkv_pad = jnp.pad(w_dkv[layer], ((0, 0), (0, pad))).astype(BF16)
    w_uq_pad = jnp.pad(w_uq[layer].reshape(q_rank, n_heads, C_NOPE + C_ROPE), ((0, 0), (0, 0), (0, pad)))
    w_uq_pad = w_uq_pad.reshape(q_rank, n_heads * (C_NOPE + V7X_LANES)).astype(BF16)
    cq, ckv, kr = _mla_down(pre[0], pre[1], _cast_bf16(w_dq, layer), q_norm[layer], w_dkv_pad, kv_norm[layer],
                            tables, kv_rank)
    qt = _mla_q_up(cq, w_uq_pad, tables, n_heads, (C_NOPE + C_ROPE) ** -0.5 * math.log2(math.e))
    k, vt = _mla_kv_up(ckv, _cast_bf16(w_ukv, layer), kr, n_heads, min(FLASH_KEY_CHUNK, s))
    o = _flash_attention(qt, k, vt)
    return _with_pre(_matmul(o, _cast_bf16(w_o, layer), residual=x, next_gain=next_gain, name="c_out_proj"))


def kernel(x, ffn1_norm, ffn1_w_gate, ffn1_w_up, ffn1_w_down, mix_norm, ffn2_norm, ffn2_w_gate, ffn2_w_up,
           ffn2_w_down, final_norm, a_w_qkv, a_w_o, b_w_pw1, b_b_pw1, b_w_dw, b_b_dw, b_ln_g, b_ln_b, b_w_pw2,
           b_b_pw2, c_w_dq, c_q_norm, c_w_uq, c_w_dkv, c_kv_norm, c_w_ukv, c_w_o):
    batch, seq, d = x.shape
    depth = ffn1_norm.shape[0]
    tables_a = [_rope_tables(seq, A_ROPE_THETA, A_ROPE_DIMS, 2, dil) for _, dil in A_GROUPS]
    tables_c = _rope_tables(seq, C_ROPE_THETA, C_ROPE, 1) if depth >= N_MIXERS else None
    outs = []
    for b in range(batch):
        xb = x[b]
        pre = _prenorm(xb, ffn1_norm[0])
        for i in range(depth):
            j = i // N_MIXERS
            kind = i % N_MIXERS
            xb, pre = _ffn(xb, pre, ffn1_w_gate, ffn1_w_up, ffn1_w_down, i, None if kind == 0 else mix_norm[i])
            if kind == 0:
                xb, pre = _mixer_a(xb, mix_norm[i], a_w_qkv, a_w_o, j, tables_a, ffn2_norm[i])
            elif kind == 1:
                xb, pre = _mixer_b(xb, pre, b_w_pw1, b_b_pw1, b_w_dw, b_b_dw, b_ln_g, b_ln_b, b_w_pw2, b_b_pw2, j,
                                   ffn2_norm[i])
            else:
                xb, pre = _mixer_c(xb, pre, c_w_dq, c_q_norm, c_w_uq, c_w_dkv, c_kv_norm, c_w_ukv, c_w_o, j,
                                   tables_c, ffn2_norm[i])
            xb, pre = _ffn(xb, pre, ffn2_w_gate, ffn2_w_up, ffn2_w_down, i, ffn1_norm[i + 1] if i + 1 < depth else None)
        outs.append(_rmsnorm(xb, final_norm, x.dtype))
    return jnp.stack(outs, axis=0)
```

```python
import functools
import math

import jax
import jax.numpy as jnp
from jax import lax
from jax.experimental import pallas as pl
from jax.experimental.pallas import tpu as pltpu

F32 = jnp.float32
BF16 = jnp.bfloat16

EPS = 1e-6
NEG = -1e30

N_MIXERS = 3
A_HEAD_DIM = 128
A_GROUPS = ((128, 1), (512, 4), (2048, 16))
A_ROPE_THETA = 500000.0
A_ROPE_DIMS = A_HEAD_DIM // 4
CONV_WIDTH = 31
C_NOPE = 128
C_ROPE = 64
C_V = 128
C_ROPE_THETA = 10000.0

V7X_VMEM_BYTES = 64 * 1024 * 1024
V7X_LANES = 128
V7X_SUBLANES = 8
COMPILER_SCRATCH_BYTES = 12 * 1024 * 1024

BAND_HALO = 64
BAND_SUB = 128
CONV_HALO = 16
FLASH_KEY_CHUNK = 512
FLASH_ONES_ROWS = 16


def _params(semantics, *buffer_bytes):
    need = sum(buffer_bytes) + COMPILER_SCRATCH_BYTES
    return pltpu.CompilerParams(dimension_semantics=semantics,
                                vmem_limit_bytes=min(need, V7X_VMEM_BYTES - 2 * 1024 * 1024))


def _nbytes(shape, dtype):
    return math.prod(shape) * jnp.dtype(dtype).itemsize


def _tile(n, target, align):
    if n <= target:
        return n
    t = (target // align) * align
    while n % t:
        t -= align
    return t


def _sigmoid(x):
    return 1.0 / (1.0 + jnp.exp(-x))


def _rmsnorm_kernel(x_ref, g_ref, o_ref):
    x = x_ref[...].astype(F32)
    y = x * lax.rsqrt(jnp.mean(x * x, axis=-1, keepdims=True) + EPS)
    o_ref[...] = (y * g_ref[...]).astype(o_ref.dtype)


def _rmsnorm(x, g, out_dtype, tm=256):
    s, d = x.shape
    tm = min(tm, s)
    return pl.pallas_call(
        _rmsnorm_kernel,
        grid=(s // tm,),
        in_specs=[pl.BlockSpec((tm, d), lambda i: (i, 0)),
                  pl.BlockSpec((1, d), lambda i: (0, 0))],
        out_specs=pl.BlockSpec((tm, d), lambda i: (i, 0)),
        out_shape=jax.ShapeDtypeStruct((s, d), out_dtype),
        compiler_params=_params(("parallel",), 2 * _nbytes((tm, d), x.dtype), 2 * _nbytes((tm, d), out_dtype),
                                2 * _nbytes((tm, d), F32)),
        name="rmsnorm",
    )(x, g.reshape(1, d).astype(F32))


def _lane_partial_sums(v):
    out = v[:, :V7X_LANES]
    for t in range(1, v.shape[1] // V7X_LANES):
        out = out + v[:, t * V7X_LANES:(t + 1) * V7X_LANES]
    return out


def _prenorm_kernel(x_ref, g_ref, xg_ref, ss_ref):
    x = x_ref[...].astype(F32)
    xg_ref[...] = (x * g_ref[...]).astype(xg_ref.dtype)
    ss_ref[...] = _lane_partial_sums(x * x)


def _prenorm(x, g, tm=256):
    s, d = x.shape
    tm = min(tm, s)
    return pl.pallas_call(
        _prenorm_kernel,
        grid=(s // tm,),
        in_specs=[pl.BlockSpec((tm, d), lambda i: (i, 0)),
                  pl.BlockSpec((1, d), lambda i: (0, 0))],
        out_specs=[pl.BlockSpec((tm, d), lambda i: (i, 0)),
                   pl.BlockSpec((tm, V7X_LANES), lambda i: (i, 0))],
        out_shape=[jax.ShapeDtypeStruct((s, d), BF16), jax.ShapeDtypeStruct((s, V7X_LANES), F32)],
        compiler_params=_params(("parallel",), 2 * _nbytes((tm, d), x.dtype), 2 * _nbytes((tm, d), BF16),
                                2 * _nbytes((tm, d), F32)),
        name="prenorm",
    )(x, g.reshape(1, d).astype(F32))


def _cast_kernel(w_ref, o_ref):
    o_ref[...] = w_ref[...].astype(o_ref.dtype)


def _cast_bf16(stack, layer, block_bytes=4 * 1024 * 1024):
    _, rows, cols = stack.shape
    tr = _tile(rows, max(16, block_bytes // (cols * 4)), 16)
    return pl.pallas_call(
        _cast_kernel,
        grid=(rows // tr,),
        in_specs=[pl.BlockSpec((None, tr, cols), lambda r: (layer, r, 0))],
        out_specs=pl.BlockSpec((tr, cols), lambda r: (r, 0)),
        out_shape=jax.ShapeDtypeStruct((rows, cols), BF16),
        compiler_params=_params(("parallel",), 2 * _nbytes((tr, cols), F32), 2 * _nbytes((tr, cols), BF16)),
        name="cast_bf16",
    )(stack)


def _rmsnorm_strided_kernel(x_ref, g_ref, *refs, dils):
    n_perm = sum(1 for dil in dils if dil > 1)
    p_refs, o_refs = refs[:n_perm], refs[n_perm:]
    x = x_ref[...].astype(F32)
    y = x * lax.rsqrt(jnp.mean(x * x, axis=-1, keepdims=True) + EPS)
    y = (y * g_ref[...]).astype(BF16)
    tm = y.shape[0]
    p_iter = iter(p_refs)
    for o_ref, dil in zip(o_refs, dils):
        if dil == 1:
            o_ref[0] = y
            continue
        yp = jnp.dot(next(p_iter)[...], y, preferred_element_type=F32).astype(BF16)
        n = tm // dil
        for r in range(dil):
            o_ref[r] = yp[r * n:(r + 1) * n]


def _rmsnorm_strided(x, g, dils, tm=256):
    s, d = x.shape
    tm = min(tm, s)
    perms, pspecs, out_specs, out_shapes = [], [], [], []
    for dil in dils:
        n = tm // dil
        assert s % tm == 0 and (dil == 1 or (tm % dil == 0 and n % 16 == 0))
        if dil > 1:
            rows = jnp.arange(tm)
            src = (rows % n) * dil + rows // n
            perms.append((src[:, None] == jnp.arange(tm)[None, :]).astype(BF16))
            pspecs.append(pl.BlockSpec((tm, tm), lambda i: (0, 0)))
        out_specs.append(pl.BlockSpec((dil, n, d), lambda i: (0, i, 0)))
        out_shapes.append(jax.ShapeDtypeStruct((dil, s // dil, d), BF16))
    return pl.pallas_call(
        functools.partial(_rmsnorm_strided_kernel, dils=tuple(dils)),
        grid=(s // tm,),
        in_specs=[pl.BlockSpec((tm, d), lambda i: (i, 0)),
                  pl.BlockSpec((1, d), lambda i: (0, 0))] + pspecs,
        out_specs=out_specs,
        out_shape=out_shapes,
        compiler_params=_params(("parallel",), 2 * _nbytes((tm, d), x.dtype),
                                2 * len(dils) * _nbytes((tm, d), BF16), 3 * _nbytes((tm, d), F32)),
        name="rmsnorm_strided",
    )(x, g.reshape(1, d).astype(F32), *perms)


def _mm_kernel(*refs, has_bias, has_res, has_next, scale):
    x_ref, w_ref = refs[0], refs[1]
    pos = 2
    acc = jnp.dot(x_ref[...], w_ref[...], preferred_element_type=F32)
    if has_bias:
        acc = acc + refs[pos][...]
        pos += 1
    if has_res:
        acc = refs[pos][...] + scale * acc
        pos += 1
    if has_next:
        gn_ref = refs[pos]
        pos += 1
    o_ref = refs[pos]
    o_ref[...] = acc.astype(o_ref.dtype)
    if has_next:
        xg_ref, ss_ref = refs[pos + 1], refs[pos + 2]
        xg_ref[...] = (acc * gn_ref[...]).astype(xg_ref.dtype)
        part = _lane_partial_sums(acc * acc)
        j = pl.program_id(1)

        @pl.when(j == 0)
        def _():
            ss_ref[...] = part

        @pl.when(j > 0)
        def _():
            ss_ref[...] = ss_ref[...] + part


def _matmul(x, w, *, bias=None, residual=None, scale=1.0, next_gain=None, out_dtype=F32, tm=1024, tn=512,
            name="matmul"):
    m, k = x.shape
    n = w.shape[1]
    tm, tn = _tile(m, tm, V7X_SUBLANES), _tile(n, tn, V7X_LANES)
    in_specs = [pl.BlockSpec((tm, k), lambda i, j: (i, 0)),
                pl.BlockSpec((k, tn), lambda i, j: (0, j))]
    args = [x, w]
    sizes = [2 * _nbytes((tm, k), x.dtype), 2 * _nbytes((k, tn), w.dtype), 2 * _nbytes((tm, tn), out_dtype),
             _nbytes((tm, tn), F32)]
    if bias is not None:
        in_specs.append(pl.BlockSpec((1, tn), lambda i, j: (0, j)))
        args.append(bias.reshape(1, n).astype(F32))
    if residual is not None:
        in_specs.append(pl.BlockSpec((tm, tn), lambda i, j: (i, j)))
        args.append(residual)
        sizes.append(2 * _nbytes((tm, tn), residual.dtype))
    out_specs = pl.BlockSpec((tm, tn), lambda i, j: (i, j))
    out_shape = jax.ShapeDtypeStruct((m, n), out_dtype)
    if next_gain is not None:
        in_specs.append(pl.BlockSpec((1, tn), lambda i, j: (0, j)))
        args.append(next_gain.reshape(1, n).astype(F32))
        out_specs = [out_specs, pl.BlockSpec((tm, tn), lambda i, j: (i, j)),
                     pl.BlockSpec((tm, V7X_LANES), lambda i, j: (i, 0))]
        out_shape = [out_shape, jax.ShapeDtypeStruct((m, n), BF16), jax.ShapeDtypeStruct((m, V7X_LANES), F32)]
        sizes += [2 * _nbytes((tm, tn), BF16), 2 * _nbytes((tm, V7X_LANES), F32), _nbytes((tm, tn), F32)]
    return pl.pallas_call(
        functools.partial(_mm_kernel, has_bias=bias is not None, has_res=residual is not None,
                          has_next=next_gain is not None, scale=scale),
        grid=(m // tm, n // tn),
        in_specs=in_specs,
        out_specs=out_specs,
        out_shape=out_shape,
        compiler_params=_params(("parallel", "arbitrary"), *sizes),
        name=name,
    )(*args)


def _row_rsqrt(ss_ref, width):
    return lax.rsqrt(jnp.sum(ss_ref[...], axis=-1, keepdims=True) * (1.0 / width) + EPS)


def _gated_kernel(*refs, kind, has_bias):
    x_ref, ss_ref, w1_ref, w2_ref = refs[:4]
    x = x_ref[...]
    r = _row_rsqrt(ss_ref, x.shape[1])
    a = jnp.dot(x, w1_ref[...], preferred_element_type=F32) * r
    b = jnp.dot(x, w2_ref[...], preferred_element_type=F32) * r
    if has_bias:
        a = a + refs[4][...]
        b = b + refs[5][...]
    o_ref = refs[-1]
    if kind == "swiglu":
        out = (a * _sigmoid(a)) * b
    else:
        out = a * _sigmoid(b)
    o_ref[...] = out.astype(o_ref.dtype)


def _gated_matmul(x, ss, w1, w2, *, kind, n_out, off2=0, b1=None, b2=None, out_dtype=BF16, tm=1024, tn=512,
                  name="gated"):
    m, k = x.shape
    tm, tn = _tile(m, tm, V7X_SUBLANES), _tile(n_out, tn, V7X_LANES)
    assert off2 % tn == 0
    ob = off2 // tn
    in_specs = [pl.BlockSpec((tm, k), lambda i, j: (i, 0)),
                pl.BlockSpec((tm, V7X_LANES), lambda i, j: (i, 0)),
                pl.BlockSpec((k, tn), lambda i, j: (0, j)),
                pl.BlockSpec((k, tn), lambda i, j: (0, j + ob))]
    args = [x, ss, w1, w2]
    has_bias = b1 is not None
    if has_bias:
        in_specs += [pl.BlockSpec((1, tn), lambda i, j: (0, j)),
                     pl.BlockSpec((1, tn), lambda i, j: (0, j + ob))]
        args += [b1.reshape(1, -1).astype(F32), b2.reshape(1, -1).astype(F32)]
    return pl.pallas_call(
        functools.partial(_gated_kernel, kind=kind, has_bias=has_bias),
        grid=(m // tm, n_out // tn),
        in_specs=in_specs,
        out_specs=pl.BlockSpec((tm, tn), lambda i, j: (i, j)),
        out_shape=jax.ShapeDtypeStruct((m, n_out), out_dtype),
        compiler_params=_params(("parallel", "arbitrary"), 2 * _nbytes((tm, k), x.dtype),
                                4 * _nbytes((k, tn), w1.dtype), 2 * _nbytes((tm, tn), out_dtype),
                                3 * _nbytes((tm, tn), F32)),
        name=name,
    )(*args)


def _rope_table_kernel(inv_ref, c_ref, sa_ref, sb_ref, *, tile, half, dil, sub_len):
    variant = pl.program_id(0)
    i = pl.program_id(1)
    shape = (tile, V7X_LANES)
    r = (i * tile) // sub_len
    l0 = (i * tile) % sub_len
    pos = ((l0 + lax.broadcasted_iota(jnp.int32, shape, 0)) * dil + r).astype(F32)
    lane = lax.broadcasted_iota(jnp.int32, shape, 1)
    ang = pos * inv_ref[...]
    cos, sin = jnp.cos(ang), jnp.sin(ang)
    rot = jnp.full(shape, variant, jnp.int32) == 0
    c_ref[...] = jnp.where(rot & (lane < 2 * half), cos, 1.0)
    sa_ref[...] = jnp.where(rot & (lane >= half) & (lane < 2 * half), sin, 0.0)
    sb_ref[...] = jnp.where(rot & (lane < half), -sin, 0.0)


def _rope_tables(seq, theta, n_rot, variants, dil=1, tile=512):
    half = n_rot // 2
    inv_freq = theta ** (-2.0 * jnp.arange(half, dtype=F32) / n_rot)
    inv = jnp.concatenate([inv_freq, inv_freq, jnp.zeros((V7X_LANES - n_rot,), F32)]).reshape(1, V7X_LANES)
    sub_len = seq // dil
    tile = min(tile, sub_len)
    assert sub_len % tile == 0
    spec = pl.BlockSpec((None, tile, V7X_LANES), lambda v, i: (v, i, 0))
    shape = jax.ShapeDtypeStruct((variants, seq, V7X_LANES), F32)
    return pl.pallas_call(
        functools.partial(_rope_table_kernel, tile=tile, half=half, dil=dil, sub_len=sub_len),
        grid=(variants, seq // tile),
        in_specs=[pl.BlockSpec((1, V7X_LANES), lambda v, i: (0, 0))],
        out_specs=[spec, spec, spec],
        out_shape=[shape, shape, shape],
        compiler_params=_params(("parallel", "parallel"), 12 * _nbytes((tile, V7X_LANES), F32)),
        name="rope_tables",
    )(inv)


def _rope_slab(t, c, sa, sb, half):
    return t * c + pltpu.roll(t, half, 1) * sa + pltpu.roll(t, V7X_LANES - half, 1) * sb


def _qkv_rope_kernel(x_ref, w_ref, c_ref, sa_ref, sb_ref, o_ref, *, half):
    acc = jnp.dot(x_ref[...], w_ref[...], preferred_element_type=F32)
    c, sa, sb = c_ref[...], sa_ref[...], sb_ref[...]
    for s in range(acc.shape[1] // V7X_LANES):
        cols = slice(s * V7X_LANES, (s + 1) * V7X_LANES)
        o_ref[:, cols] = _rope_slab(acc[:, cols], c, sa, sb, half).astype(o_ref.dtype)


def _qkv_rope(x, w, tables, width, group, *, tm=1024, tn=1024):
    m, k = x.shape
    n = 3 * width
    tm, tn = _tile(m, tm, V7X_SUBLANES), _tile(width, tn, V7X_LANES)
    per = width // tn
    col0 = group * 3 * per

    def tmap(i, j):
        return (jnp.where(j // per == 2, 1, 0), i, 0)

    tspec = pl.BlockSpec((None, tm, V7X_LANES), tmap)
    return pl.pallas_call(
        functools.partial(_qkv_rope_kernel, half=A_ROPE_DIMS // 2),
        grid=(m // tm, n // tn),
        in_specs=[pl.BlockSpec((tm, k), lambda i, j: (i, 0)),
                  pl.BlockSpec((k, tn), lambda i, j: (0, j + col0)),
                  tspec, tspec, tspec],
        out_specs=pl.BlockSpec((tm, tn), lambda i, j: (i, j)),
        out_shape=jax.ShapeDtypeStruct((m, n), BF16),
        compiler_params=_params(("parallel", "arbitrary"), 2 * _nbytes((tm, k), x.dtype),
                                2 * _nbytes((k, tn), w.dtype), 2 * _nbytes((tm, tn), BF16),
                                2 * _nbytes((tm, tn), F32), 6 * _nbytes((tm, V7X_LANES), F32)),
        name="qkv_rope",
    )(x, w, *tables)


def _band_kernel(q_ref, kp_ref, k_ref, kn_ref, vp_ref, v_ref, vn_ref, o_ref, lse_ref, *,
                 tq, n_heads, dh, half_w, sub_len, scale):
    i = pl.program_id(1)
    n_sub = tq // BAND_SUB
    win = BAND_SUB + 2 * BAND_HALO
    row = lax.broadcasted_iota(jnp.int32, (BAND_SUB, win), 0)
    col = lax.broadcasted_iota(jnp.int32, (BAND_SUB, win), 1)
    in_band = jnp.abs(col - BAND_HALO - row) <= half_w
    lane = lax.broadcasted_iota(jnp.int32, (BAND_SUB, V7X_LANES), 1)
    valid = []
    for j in range(n_sub):
        kpos = i * tq + (j * BAND_SUB - BAND_HALO) + col
        valid.append(in_band & (kpos >= 0) & (kpos < sub_len))
    lse_acc = [jnp.zeros((BAND_SUB, V7X_LANES), F32) for _ in range(n_sub)]
    for h in range(n_heads):
        hs = slice(h * dh, (h + 1) * dh)
        kc = jnp.concatenate([kp_ref[:, hs], k_ref[:, hs], kn_ref[:, hs]], axis=0)
        vc = jnp.concatenate([vp_ref[:, hs], v_ref[:, hs], vn_ref[:, hs]], axis=0)
        for j in range(n_sub):
            rows = slice(j * BAND_SUB, (j + 1) * BAND_SUB)
            q = q_ref[rows, hs]
            kw = kc[j * BAND_SUB:j * BAND_SUB + win]
            vw = vc[j * BAND_SUB:j * BAND_SUB + win]
            s = lax.dot_general(q, kw, (((1,), (1,)), ((), ())), preferred_element_type=F32) * scale
            s = jnp.where(valid[j], s, NEG)
            m = jnp.max(s, axis=-1, keepdims=True)
            e = jnp.exp(s - m)
            l = jnp.sum(e, axis=-1, keepdims=True)
            o = jnp.dot(e.astype(vw.dtype), vw, preferred_element_type=F32)
            o_ref[rows, hs] = (o * (1.0 / l)).astype(o_ref.dtype)
            lse_acc[j] = jnp.where(lane == h, m + jnp.log(l), lse_acc[j])
    for j in range(n_sub):
        lse_ref[j * BAND_SUB:(j + 1) * BAND_SUB, :] = lse_acc[j]


def _band_attention(qkv, dil, half_w, *, n_heads, dh, tq=256):
    dil_, sub_len, w = qkv.shape
    width = n_heads * dh
    tq = min(tq, sub_len)
    assert dil_ == dil and w == 3 * width
    assert sub_len % tq == 0 and tq % BAND_SUB == 0 and half_w <= BAND_HALO and n_heads <= V7X_LANES
    hb = tq // BAND_HALO
    last = sub_len // BAND_HALO - 1

    def main(sec):
        return pl.BlockSpec((None, tq, width), lambda r, i: (r, i, sec))

    def prev(sec):
        return pl.BlockSpec((None, BAND_HALO, width), lambda r, i: (r, jnp.maximum(i * hb - 1, 0), sec))

    def nxt(sec):
        return pl.BlockSpec((None, BAND_HALO, width), lambda r, i: (r, jnp.minimum((i + 1) * hb, last), sec))

    return pl.pallas_call(
        functools.partial(_band_kernel, tq=tq, n_heads=n_heads, dh=dh, half_w=half_w, sub_len=sub_len,
                          scale=dh ** -0.5),
        grid=(dil, sub_len // tq),
        in_specs=[main(0), prev(1), main(1), nxt(1), prev(2), main(2), nxt(2)],
        out_specs=[pl.BlockSpec((None, tq, width), lambda r, i: (r, i, 0)),
                   pl.BlockSpec((None, tq, V7X_LANES), lambda r, i: (r, i, 0))],
        out_shape=[jax.ShapeDtypeStruct((dil, sub_len, width), BF16),
                   jax.ShapeDtypeStruct((dil, sub_len, V7X_LANES), F32)],
        compiler_params=_params(("parallel", "arbitrary"), 6 * _nbytes((tq + 2 * BAND_HALO, width), qkv.dtype),
                                2 * _nbytes((tq, width), F32), 2 * _nbytes((tq, V7X_LANES), F32)),
        name=f"band_attention_d{dil}",
    )(qkv, qkv, qkv, qkv, qkv, qkv, qkv)


def _merge_kernel(*refs, dils, n_heads, dh):
    n_groups = len(dils)
    o_refs, l_refs, out_ref = refs[:n_groups], refs[n_groups:2 * n_groups], refs[2 * n_groups]
    o_nat, l_nat = refs[2 * n_groups + 1], refs[2 * n_groups + 2]
    tm = out_ref.shape[0]
    for g, dil in enumerate(dils):
        n = tm // dil
        for r in range(dil):
            rows = pl.ds(r, n, stride=dil) if dil > 1 else slice(0, tm)
            l_nat[g, rows, :] = l_refs[g][r]
            for h in range(n_heads):
                o_nat[g * n_heads + h, rows, :] = o_refs[g][r, :, h * dh:(h + 1) * dh].astype(F32)
    ls = [l_nat[g] for g in range(n_groups)]
    m = functools.reduce(jnp.maximum, ls)
    es = [jnp.exp(l - m) for l in ls]
    inv = 1.0 / functools.reduce(lambda a, b: a + b, es)
    ws = [e * inv for e in es]
    for h in range(n_heads):
        acc = ws[0][:, h:h + 1] * o_nat[h]
        for g in range(1, n_groups):
            acc = acc + ws[g][:, h:h + 1] * o_nat[g * n_heads + h]
        out_ref[:, h * dh:(h + 1) * dh] = acc.astype(out_ref.dtype)


def _merge_groups(outs, lses, dils, *, n_heads, dh, tm=256):
    width = n_heads * dh
    s = outs[0].shape[0] * outs[0].shape[1]
    tm = min(tm, s)
    n = len(outs)
    assert dh == V7X_LANES and all(tm % dil == 0 and (tm // dil) % V7X_SUBLANES == 0 for dil in dils)
    ospecs = [pl.BlockSpec((dil, tm // dil, width), lambda i: (0, i, 0)) for dil in dils]
    lspecs = [pl.BlockSpec((dil, tm // dil, V7X_LANES), lambda i: (0, i, 0)) for dil in dils]
    return pl.pallas_call(
        functools.partial(_merge_kernel, dils=tuple(dils), n_heads=n_heads, dh=dh),
        grid=(s // tm,),
        in_specs=ospecs + lspecs,
        out_specs=pl.BlockSpec((tm, width), lambda i: (i, 0)),
        out_shape=jax.ShapeDtypeStruct((s, width), BF16),
        scratch_shapes=[pltpu.VMEM((n * n_heads, tm, V7X_LANES), F32), pltpu.VMEM((n, tm, V7X_LANES), F32)],
        compiler_params=_params(("parallel",), 3 * n * _nbytes((tm, width), F32), 2 * _nbytes((tm, width), BF16),
                                3 * n * _nbytes((tm, V7X_LANES), F32)),
        name="merge_groups",
    )(*outs, *lses)


def _conv_ln_kernel(up_ref, u_ref, un_ref, w_ref, bdw_ref, g_ref, b_ref, o_ref, win_ref, y_ref, *,
                    ts, kw, rc, rl):
    i, j = pl.program_id(0), pl.program_id(1)
    n_i, n_j = pl.num_programs(0), pl.num_programs(1)
    cc = u_ref.shape[1]
    first = CONV_HALO - kw // 2
    rows = ts + 2 * CONV_HALO
    win_ref[0, 0:CONV_HALO, :] = jnp.where(i > 0, up_ref[...], 0.0)
    win_ref[0, CONV_HALO:CONV_HALO + ts, :] = u_ref[...]
    win_ref[0, CONV_HALO + ts:, :] = jnp.where(i < n_i - 1, un_ref[...], 0.0)
    win = win_ref[0]
    for p in range(1, V7X_SUBLANES):
        win_ref[p] = pltpu.roll(win, rows - p, 0)
    groups = rc // V7X_SUBLANES
    for r0 in range(0, ts, rc):
        acc = jnp.broadcast_to(bdw_ref[...], (groups, V7X_SUBLANES, cc))
        for k in range(kw):
            p = (first + k) % V7X_SUBLANES
            a = r0 + first + k - p
            wk = w_ref[k * V7X_SUBLANES:(k + 1) * V7X_SUBLANES, :]
            acc = acc + win_ref[p, a:a + rc, :].reshape(groups, V7X_SUBLANES, cc) * wk[None]
        y_ref[j, r0:r0 + rc, :] = acc.reshape(rc, cc)

    @pl.when(j == n_j - 1)
    def _():
        n_c = y_ref.shape[0]
        inv_d = 1.0 / (n_c * cc)

        def ln_rows(rb, carry):
            r0 = pl.multiple_of(rb * rl, rl)
            ys = [y_ref[c, pl.ds(r0, rl), :] for c in range(n_c)]
            mu = functools.reduce(lambda a, b: a + b, [jnp.sum(y, axis=-1, keepdims=True) for y in ys]) * inv_d
            ycs = [y - mu for y in ys]
            var = functools.reduce(lambda a, b: a + b, [jnp.sum(y * y, axis=-1, keepdims=True) for y in ycs]) * inv_d
            inv = lax.rsqrt(var + EPS)
            for c in range(n_c):
                cols = slice(c * cc, (c + 1) * cc)
                z = ycs[c] * inv * g_ref[:, cols] + b_ref[:, cols]
                o_ref[pl.ds(r0, rl), cols] = (z * _sigmoid(z)).astype(o_ref.dtype)
            return carry

        lax.fori_loop(0, ts // rl, ln_rows, 0)


def _conv_ln_swish(u, w_dw, b_dw, ln_g, ln_b, *, ts=256, cc=512):
    s, d = u.shape
    kw = w_dw.shape[0]
    ts, cc = min(ts, s), min(cc, d)
    assert kw // 2 <= CONV_HALO and ts % CONV_HALO == 0 and s % ts == 0 and d % cc == 0
    hb = ts // CONV_HALO
    last = s // CONV_HALO - 1
    rc, rl = min(64, ts), min(16, ts)
    vec = pl.BlockSpec((1, d), lambda i, j: (0, 0))
    return pl.pallas_call(
        functools.partial(_conv_ln_kernel, ts=ts, kw=kw, rc=rc, rl=rl),
        grid=(s // ts, d // cc),
        in_specs=[pl.BlockSpec((CONV_HALO, cc), lambda i, j: (jnp.maximum(i * hb - 1, 0), j)),
                  pl.BlockSpec((ts, cc), lambda i, j: (i, j)),
                  pl.BlockSpec((CONV_HALO, cc), lambda i, j: (jnp.minimum((i + 1) * hb, last), j)),
                  pl.BlockSpec((kw * V7X_SUBLANES, cc), lambda i, j: (0, j)),
                  pl.BlockSpec((1, cc), lambda i, j: (0, j)),
                  vec, vec],
        out_specs=pl.BlockSpec((ts, d), lambda i, j: (i, 0)),
        out_shape=jax.ShapeDtypeStruct((s, d), BF16),
        scratch_shapes=[pltpu.VMEM((V7X_SUBLANES, ts + 2 * CONV_HALO, cc), F32), pltpu.VMEM((d // cc, ts, cc), F32)],
        compiler_params=_params(("parallel", "arbitrary"), (2 + V7X_SUBLANES) * _nbytes((ts + 2 * CONV_HALO, cc), F32),
                                2 * _nbytes((ts, d), BF16), _nbytes((ts, d), F32),
                                2 * _nbytes((kw + 3 * V7X_SUBLANES, d), F32)),
        name="conv_ln_swish",
    )(u, u, u, jnp.repeat(w_dw.astype(F32), V7X_SUBLANES, axis=0), b_dw.reshape(1, d).astype(F32),
      ln_g.reshape(1, d).astype(F32), ln_b.reshape(1, d).astype(F32))


def _mla_down_kernel(x_ref, ss_ref, wq_ref, wkv_ref, qn_ref, kvn_ref, c_ref, sa_ref, sb_ref, cq_ref, ckv_ref, kr_ref,
                     *, kv_rank, half):
    x = x_ref[...]
    r = _row_rsqrt(ss_ref, x.shape[1])
    a = jnp.dot(x, wq_ref[...], preferred_element_type=F32) * r
    a = a * lax.rsqrt(jnp.mean(a * a, axis=-1, keepdims=True) + EPS)
    cq_ref[...] = (a * qn_ref[...]).astype(cq_ref.dtype)
    b = jnp.dot(x, wkv_ref[...], preferred_element_type=F32) * r
    c = b[:, :kv_rank]
    c = c * lax.rsqrt(jnp.mean(c * c, axis=-1, keepdims=True) + EPS)
    ckv_ref[...] = (c * kvn_ref[...]).astype(ckv_ref.dtype)
    t = b[:, kv_rank:]
    kr_ref[...] = _rope_slab(t, c_ref[...], sa_ref[...], sb_ref[...], half).astype(kr_ref.dtype)


def _mla_down(x, ss, w_dq, q_norm, w_dkv_pad, kv_norm, tables, kv_rank, *, tm=512):
    m, k = x.shape
    qr = w_dq.shape[1]
    nkv = w_dkv_pad.shape[1]
    tm = min(tm, m)
    tspec = pl.BlockSpec((None, tm, V7X_LANES), lambda i: (0, i, 0))
    return pl.pallas_call(
        functools.partial(_mla_down_kernel, kv_rank=kv_rank, half=C_ROPE // 2),
        grid=(m // tm,),
        in_specs=[pl.BlockSpec((tm, k), lambda i: (i, 0)),
                  pl.BlockSpec((tm, V7X_LANES), lambda i: (i, 0)),
                  pl.BlockSpec((k, qr), lambda i: (0, 0)),
                  pl.BlockSpec((k, nkv), lambda i: (0, 0)),
                  pl.BlockSpec((1, qr), lambda i: (0, 0)),
                  pl.BlockSpec((1, kv_rank), lambda i: (0, 0)),
                  tspec, tspec, tspec],
        out_specs=[pl.BlockSpec((tm, qr), lambda i: (i, 0)),
                   pl.BlockSpec((tm, kv_rank), lambda i: (i, 0)),
                   pl.BlockSpec((tm, V7X_LANES), lambda i: (i, 0))],
        out_shape=[jax.ShapeDtypeStruct((m, qr), BF16),
                   jax.ShapeDtypeStruct((m, kv_rank), BF16),
                   jax.ShapeDtypeStruct((m, V7X_LANES), BF16)],
        compiler_params=_params(("parallel",), 2 * _nbytes((tm, k), x.dtype), 2 * _nbytes((k, qr + nkv), BF16),
                                2 * _nbytes((tm, qr + nkv), BF16), 2 * _nbytes((tm, qr + nkv), F32),
                                6 * _nbytes((tm, V7X_LANES), F32)),
        name="mla_down",
    )(x, ss, w_dq, w_dkv_pad, q_norm.reshape(1, qr).astype(F32), kv_norm.reshape(1, kv_rank).astype(F32), *tables)


def _mla_q_up_kernel(cq_ref, w_ref, c_ref, sa_ref, sb_ref, qt_ref, *, half, scale):
    r = jnp.dot(cq_ref[...], w_ref[...], preferred_element_type=F32)
    hw = qt_ref.shape[1]
    for e in range(qt_ref.shape[0]):
        nope = r[:, e * hw:e * hw + C_NOPE] * scale
        rope = _rope_slab(r[:, e * hw + C_NOPE:(e + 1) * hw], c_ref[...], sa_ref[...], sb_ref[...], half) * scale
        qt_ref[e, :C_NOPE, :] = nope.T.astype(qt_ref.dtype)
        qt_ref[e, C_NOPE:, :] = rope.T.astype(qt_ref.dtype)


def _mla_q_up(cq, w_uq_pad, tables, n_heads, scale, *, tm=1024, heads_per_step=2):
    m, k = cq.shape
    hw = w_uq_pad.shape[1] // n_heads
    tm = min(tm, m)
    hp = heads_per_step if n_heads % heads_per_step == 0 else 1
    tspec = pl.BlockSpec((None, tm, V7X_LANES), lambda i, h: (0, i, 0))
    return pl.pallas_call(
        functools.partial(_mla_q_up_kernel, half=C_ROPE // 2, scale=scale),
        grid=(m // tm, n_heads // hp),
        in_specs=[pl.BlockSpec((tm, k), lambda i, h: (i, 0)),
                  pl.BlockSpec((k, hp * hw), lambda i, h: (0, h)),
                  tspec, tspec, tspec],
        out_specs=pl.BlockSpec((hp, hw, tm), lambda i, h: (h, 0, i)),
        out_shape=jax.ShapeDtypeStruct((n_heads, hw, m), BF16),
        compiler_params=_params(("parallel", "arbitrary"), 2 * _nbytes((tm, k), BF16),
                                2 * _nbytes((k, hp * hw), BF16), 2 * _nbytes((tm, hp * hw), BF16),
                                3 * _nbytes((tm, hp * hw), F32), 6 * _nbytes((tm, V7X_LANES), F32)),
        name="mla_q_up",
    )(cq, w_uq_pad, *tables)


def _mla_kv_up_kernel(c_ref, w_ref, kr_ref, k_ref, vt_ref):
    r = jnp.dot(c_ref[...], w_ref[...], preferred_element_type=F32)
    hw = C_NOPE + C_V
    n_chunks, _, tk = vt_ref.shape[1:]
    for e in range(k_ref.shape[0]):
        k_ref[e, :, :C_NOPE] = r[:, e * hw:e * hw + C_NOPE].astype(k_ref.dtype)
        k_ref[e, :, C_NOPE:] = kr_ref[...]
        vt = r[:, e * hw + C_NOPE:(e + 1) * hw].T.astype(vt_ref.dtype)
        for c in range(n_chunks):
            vt_ref[e, c, :C_V, :] = vt[:, c * tk:(c + 1) * tk]
            vt_ref[e, c, C_V:, :] = jnp.ones((FLASH_ONES_ROWS, tk), vt_ref.dtype)


def _mla_kv_up(c, w_ukv, kr, n_heads, tk, *, tm=1024, heads_per_step=2):
    m, k = c.shape
    hw = w_ukv.shape[1] // n_heads
    tm = min(tm, m)
    hp = heads_per_step if n_heads % heads_per_step == 0 else 1
    assert tm % tk == 0 and m % tm == 0 and hw == C_NOPE + C_V
    dva = C_V + FLASH_ONES_ROWS
    return pl.pallas_call(
        _mla_kv_up_kernel,
        grid=(m // tm, n_heads // hp),
        in_specs=[pl.BlockSpec((tm, k), lambda i, h: (i, 0)),
                  pl.BlockSpec((k, hp * hw), lambda i, h: (0, h)),
                  pl.BlockSpec((tm, V7X_LANES), lambda i, h: (i, 0))],
        out_specs=[pl.BlockSpec((hp, tm, C_NOPE + V7X_LANES), lambda i, h: (h, i, 0)),
                   pl.BlockSpec((hp, tm // tk, dva, tk), lambda i, h: (h, i, 0, 0))],
        out_shape=[jax.ShapeDtypeStruct((n_heads, m, C_NOPE + V7X_LANES), BF16),
                   jax.ShapeDtypeStruct((n_heads, m // tk, dva, tk), BF16)],
        compiler_params=_params(("parallel", "arbitrary"), 2 * _nbytes((tm, k), BF16),
                                2 * _nbytes((k, hp * hw), BF16), 2 * hp * _nbytes((tm, 3 * V7X_LANES), BF16),
                                3 * _nbytes((tm, hp * hw), F32), 2 * _nbytes((tm, V7X_LANES), BF16)),
        name="mla_kv_up",
    )(c, w_ukv, kr)


def _flash_kernel(qt_ref, k_ref, vt_ref, o_ref, s_ref, m_ref, acc_ref, *, tk, qs, dv):
    n_kv = k_ref.shape[0] // tk
    tq = qt_ref.shape[1]
    m_ref[...] = jnp.full(m_ref.shape, NEG, F32)
    acc_ref[...] = jnp.zeros(acc_ref.shape, F32)
    s_ref[...] = jnp.dot(k_ref[pl.ds(0, tk), :], qt_ref[...], preferred_element_type=F32)

    def step(kk, carry):
        nxt = jnp.minimum(kk + 1, n_kv - 1)
        kc = k_ref[pl.ds(pl.multiple_of(nxt * tk, tk), tk), :]
        vc = vt_ref[kk]
        for j in range(tq // qs):
            cols = slice(j * qs, (j + 1) * qs)
            s_next = jnp.dot(kc, qt_ref[:, cols], preferred_element_type=F32)
            st = s_ref[:, cols]
            m_old = m_ref[:, cols]
            m_new = jnp.maximum(m_old, jnp.max(st, axis=0, keepdims=True))
            alpha = jnp.exp2(m_old - m_new)
            pt = jnp.exp2(st - m_new)
            acc_ref[:, cols] = alpha * acc_ref[:, cols] + jnp.dot(vc, pt.astype(vc.dtype),
                                                                  preferred_element_type=F32)
            m_ref[:, cols] = m_new
            s_ref[:, cols] = s_next
        return carry

    lax.fori_loop(0, n_kv, step, 0)
    o = acc_ref[:dv, :] * (1.0 / acc_ref[dv:dv + 1, :])
    o_ref[...] = o.T.astype(o_ref.dtype)


def _flash_attention(qt, k, vt, *, tq=4096, qs=512):
    n_heads, dk, s = qt.shape
    n_chunks, dva, tk = vt.shape[1:]
    dv = dva - FLASH_ONES_ROWS
    tq = min(tq, s)
    qs = min(qs, tq)
    assert n_chunks * tk == s and s % tq == 0 and tq % qs == 0
    return pl.pallas_call(
        functools.partial(_flash_kernel, tk=tk, qs=qs, dv=dv),
        grid=(n_heads, s // tq),
        in_specs=[pl.BlockSpec((None, dk, tq), lambda h, i: (h, 0, i)),
                  pl.BlockSpec((None, s, dk), lambda h, i: (h, 0, 0)),
                  pl.BlockSpec((None, n_chunks, dva, tk), lambda h, i: (h, 0, 0, 0))],
        out_specs=pl.BlockSpec((tq, dv), lambda h, i: (i, h)),
        out_shape=jax.ShapeDtypeStruct((s, n_heads * dv), BF16),
        scratch_shapes=[pltpu.VMEM((tk, tq), F32), pltpu.VMEM((1, tq), F32), pltpu.VMEM((dva, tq), F32)],
        compiler_params=_params(("parallel", "arbitrary"), 2 * _nbytes((dk, tq), BF16),
                                2 * _nbytes((s, dk + dva), BF16), 2 * _nbytes((tq, dv), BF16),
                                _nbytes((tk + dva + V7X_SUBLANES, tq), F32), 2 * _nbytes((tk, qs), F32)),
        name="mla_flash_attention",
    )(qt, k, vt)


def _with_pre(res):
    if isinstance(res, (list, tuple)):
        return res[0], (res[1], res[2])
    return res, None


def _ffn(x, pre, w_gate, w_up, w_down, layer, next_gain):
    a = _gated_matmul(pre[0], pre[1], _cast_bf16(w_gate, layer), _cast_bf16(w_up, layer), kind="swiglu",
                      n_out=w_gate.shape[2], name="ffn_gate_up")
    return _with_pre(_matmul(a, _cast_bf16(w_down, layer), residual=x, scale=0.5, next_gain=next_gain,
                             name="ffn_down"))


def _mixer_a(x, norm_g, w_qkv, w_o, layer, tables, next_gain):
    s, d = x.shape
    n_heads = d // (2 * A_HEAD_DIM)
    width = n_heads * A_HEAD_DIM
    dils = [dil for _, dil in A_GROUPS]
    hs = _rmsnorm_strided(x, norm_g, dils)
    w = _cast_bf16(w_qkv, layer)
    outs, lses = [], []
    for g, (window, dil) in enumerate(A_GROUPS):
        qkv = _qkv_rope(hs[g].reshape(s, d), w, tables[g], width, g)
        o, lse = _band_attention(qkv.reshape(dil, s // dil, 3 * width), dil, window // (2 * dil),
                                 n_heads=n_heads, dh=A_HEAD_DIM)
        outs.append(o)
        lses.append(lse)
    merged = _merge_groups(outs, lses, dils, n_heads=n_heads, dh=A_HEAD_DIM)
    return _with_pre(_matmul(merged, _cast_bf16(w_o, layer), residual=x, next_gain=next_gain, name="a_out_proj"))


def _mixer_b(x, pre, w_pw1, b_pw1, w_dw, b_dw, ln_g, ln_b, w_pw2, b_pw2, layer, next_gain):
    d = x.shape[1]
    w1 = _cast_bf16(w_pw1, layer)
    u = _gated_matmul(pre[0], pre[1], w1, w1, kind="glu", n_out=d, off2=d, b1=b_pw1[layer], b2=b_pw1[layer],
                      out_dtype=F32, name="conv_pw1_glu")
    v = _conv_ln_swish(u, w_dw[layer], b_dw[layer], ln_g[layer], ln_b[layer])
    return _with_pre(_matmul(v, _cast_bf16(w_pw2, layer), bias=b_pw2[layer], residual=x, next_gain=next_gain,
                             name="conv_pw2"))


def _mixer_c(x, pre, w_dq, q_norm, w_uq, w_dkv, kv_norm, w_ukv, w_o, layer, tables, next_gain):
    s, d = x.shape
    n_heads = d // C_V
    kv_rank = w_dkv.shape[2] - C_ROPE
    q_rank = w_dq.shape[2]
    pad = V7X_LANES - C_ROPE
    w_dkv_pad = jnp.pad(w_dkv[layer], ((0, 0), (0, pad))).astype(BF16)
    w_uq_pad = jnp.pad(w_uq[layer].reshape(q_rank, n_heads, C_NOPE + C_ROPE), ((0, 0), (0, 0), (0, pad)))
    w_uq_pad = w_uq_pad.reshape(q_rank, n_heads * (C_NOPE + V7X_LANES)).astype(BF16)
    cq, ckv, kr = _mla_down(pre[0], pre[1], _cast_bf16(w_dq, layer), q_norm[layer], w_dkv_pad, kv_norm[layer],
                            tables, kv_rank)
    qt = _mla_q_up(cq, w_uq_pad, tables, n_heads, (C_NOPE + C_ROPE) ** -0.5 * math.log2(math.e))
    k, vt = _mla_kv_up(ckv, _cast_bf16(w_ukv, layer), kr, n_heads, min(FLASH_KEY_CHUNK, s))
    o = _flash_attention(qt, k, vt)
    return _with_pre(_matmul(o, _cast_bf16(w_o, layer), residual=x, next_gain=next_gain, name="c_out_proj"))


def kernel(x, ffn1_norm, ffn1_w_gate, ffn1_w_up, ffn1_w_down, mix_norm, ffn2_norm, ffn2_w_gate, ffn2_w_up,
           ffn2_w_down, final_norm, a_w_qkv, a_w_o, b_w_pw1, b_b_pw1, b_w_dw, b_b_dw, b_ln_g, b_ln_b, b_w_pw2,
           b_b_pw2, c_w_dq, c_q_norm, c_w_uq, c_w_dkv, c_kv_norm, c_w_ukv, c_w_o):
    batch, seq, d = x.shape
    depth = ffn1_norm.shape[0]
    tables_a = [_rope_tables(seq, A_ROPE_THETA, A_ROPE_DIMS, 2, dil) for _, dil in A_GROUPS]
    tables_c = _rope_tables(seq, C_ROPE_THETA, C_ROPE, 1) if depth >= N_MIXERS else None
    outs = []
    for b in range(batch):
        xb = x[b]
        pre = _prenorm(xb, ffn1_norm[0])
        for i in range(depth):
            j = i // N_MIXERS
            kind = i % N_MIXERS
            xb, pre = _ffn(xb, pre, ffn1_w_gate, ffn1_w_up, ffn1_w_down, i, None if kind == 0 else mix_norm[i])
            if kind == 0:
                xb, pre = _mixer_a(xb, mix_norm[i], a_w_qkv, a_w_o, j, tables_a, ffn2_norm[i])
            elif kind == 1:
                xb, pre = _mixer_b(xb, pre, b_w_pw1, b_b_pw1, b_w_dw, b_b_dw, b_ln_g, b_ln_b, b_w_pw2, b_b_pw2, j,
                                   ffn2_norm[i])
            else:
                xb, pre = _mixer_c(xb, pre, c_w_dq, c_q_norm, c_w_uq, c_w_dkv, c_kv_norm, c_w_ukv, c_w_o, j,
                                   tables_c, ffn2_norm[i])
            xb, pre = _ffn(xb, pre, ffn2_w_gate, ffn2_w_up, ffn2_w_down, i, ffn1_norm[i + 1] if i + 1 < depth else None)
        outs.append(_rmsnorm(xb, final_norm, x.dtype))
    return jnp.stack(outs, axis=0)
```
